```python
import jax, jax.numpy as jnp
from jax import lax
import numpy as np

D_MODEL = 2048
BATCH = 8
SEQ = 2048
DEPTH = 2
DEC_BATCH = 2
DEC_SEQ = 16384
PAST_LEN = 128

W_F = D_MODEL // 2
F_GROUPS = 8
F_GROUP_DIM = W_F // F_GROUPS
W_C = D_MODEL // 2
C_GROUPS = 8
CONV_W = 3
N_BRANCH = 2
N_IN = W_F + 3 * W_C + N_BRANCH * D_MODEL
D_FF = 5632
N_EXPERTS = 8
TOP_K = 2
D_FF_E = 7168
N_DENSE = (DEPTH + 1) // 2
N_MOE = DEPTH // 2
EPS = 1e-6

kernel_name = "fourier_shortconv_gated_hybrid_encoder"


def rmsnorm(x, g):
    x32 = x.astype(jnp.float32)
    y = x32 * lax.rsqrt(jnp.mean(x32 * x32, axis=-1, keepdims=True) + EPS) * g.astype(jnp.float32)
    return y.astype(x.dtype)


def swiglu(h, wg, wu, wd):
    return (jax.nn.silu(h @ wg) * (h @ wu)) @ wd


def mixer(h, w_in, conv_w, conv_b, w_fourier, w_conv_out, w_o):
    bsz, s, _ = h.shape
    z = h @ w_in
    o1 = W_F
    o2 = o1 + W_C
    o3 = o2 + W_C
    o4 = o3 + W_C
    zf = z[..., :o1]
    zb = z[..., o1:o2]
    zc = z[..., o2:o3]
    zv = z[..., o3:o4]
    zg = z[..., o4:]
    zf = zf.reshape(bsz, s, F_GROUPS, F_GROUP_DIM).astype(jnp.float32)
    yf = jnp.fft.fftn(zf, axes=(1, 3), norm="ortho").real
    yf = yf.reshape(bsz, s, W_F).astype(h.dtype)
    u = zc * zv
    up = jnp.pad(u, ((0, 0), (1, 1), (0, 0)))
    cv = up[:, :-2] * conv_w[0] + up[:, 1:-1] * conv_w[1] + up[:, 2:] * conv_w[2] + conv_b
    yc = zb * cv
    gates = jax.nn.sigmoid(zg.astype(jnp.float32)).astype(h.dtype).reshape(bsz, s, N_BRANCH, D_MODEL)
    m = gates[:, :, 0] * (yf @ w_fourier) + gates[:, :, 1] * (yc @ w_conv_out)
    return m @ w_o


def moe(h, router_w, wg, wu, wd):
    logits = (h @ router_w).astype(jnp.float32)
    top_v, top_i = lax.top_k(logits, TOP_K)
    probs = jax.nn.softmax(top_v, axis=-1)
    combine = jnp.sum(jax.nn.one_hot(top_i, N_EXPERTS, dtype=jnp.float32) * probs[..., None], axis=-2)
    combine = combine.astype(h.dtype)
    out = jnp.zeros_like(h)
    for e in range(N_EXPERTS):
        out = out + combine[..., e:e + 1] * swiglu(h, wg[e], wu[e], wd[e])
    return out


def trunk(x, norm1_g, w_in, conv_w, conv_b, w_fourier, w_conv_out, w_o, norm2_g,
          dense_w_gate, dense_w_up, dense_w_down, router_w, moe_w_gate, moe_w_up, moe_w_down, final_g):
    for l in range(DEPTH):
        h = rmsnorm(x, norm1_g[l])
        x = x + mixer(h, w_in[l], conv_w[l], conv_b[l], w_fourier[l], w_conv_out[l], w_o[l])
        h = rmsnorm(x, norm2_g[l])
        j = l // 2
        if l % 2 == 0:
            x = x + swiglu(h, dense_w_gate[j], dense_w_up[j], dense_w_down[j])
        else:
            x = x + moe(h, router_w[j], moe_w_gate[j], moe_w_up[j], moe_w_down[j])
    return rmsnorm(x, final_g)


def setup_inputs(seed: int = 0) -> dict:
    key = jax.random.key(seed)
    ks = jax.random.split(key, 20)
    f32 = jnp.float32

    def nrm(k, shape, fan_in):
        return jax.random.normal(k, shape, f32) * (fan_in ** -0.5)

    return {
        "x_prompt": jax.random.normal(ks[0], (BATCH, SEQ, D_MODEL), f32),
        "x_sample": jax.random.normal(ks[1], (DEC_BATCH, DEC_SEQ, D_MODEL), f32),
        "norm1_g": 1.0 + 0.02 * jax.random.normal(ks[2], (DEPTH, D_MODEL), f32),
        "w_in": nrm(ks[3], (DEPTH, D_MODEL, N_IN), D_MODEL),
        "conv_w": nrm(ks[4], (DEPTH, CONV_W, W_C), CONV_W),
        "conv_b": 0.02 * jax.random.normal(ks[5], (DEPTH, W_C), f32),
        "w_fourier": nrm(ks[6], (DEPTH, W_F, D_MODEL), W_F),
        "w_conv_out": nrm(ks[7], (DEPTH, W_C, D_MODEL), W_C),
        "w_o": nrm(ks[8], (DEPTH, D_MODEL, D_MODEL), D_MODEL),
        "norm2_g": 1.0 + 0.02 * jax.random.normal(ks[9], (DEPTH, D_MODEL), f32),
        "dense_w_gate": nrm(ks[10], (N_DENSE, D_MODEL, D_FF), D_MODEL),
        "dense_w_up": nrm(ks[11], (N_DENSE, D_MODEL, D_FF), D_MODEL),
        "dense_w_down": nrm(ks[12], (N_DENSE, D_FF, D_MODEL), D_FF),
        "router_w": nrm(ks[13], (N_MOE, D_MODEL, N_EXPERTS), D_MODEL),
        "moe_w_gate": nrm(ks[14], (N_MOE, N_EXPERTS, D_MODEL, D_FF_E), D_MODEL),
        "moe_w_up": nrm(ks[15], (N_MOE, N_EXPERTS, D_MODEL, D_FF_E), D_MODEL),
        "moe_w_down": nrm(ks[16], (N_MOE, N_EXPERTS, D_FF_E, D_MODEL), D_FF_E),
        "final_g": 1.0 + 0.02 * jax.random.normal(ks[17], (D_MODEL,), f32),
    }


def reference(x_prompt, x_sample, norm1_g, w_in, conv_w, conv_b, w_fourier, w_conv_out, w_o, norm2_g,
              dense_w_gate, dense_w_up, dense_w_down, router_w, moe_w_gate, moe_w_up, moe_w_down, final_g):
    y_prompt = trunk(x_prompt, norm1_g, w_in, conv_w, conv_b, w_fourier, w_conv_out, w_o, norm2_g,
                     dense_w_gate, dense_w_up, dense_w_down, router_w, moe_w_gate, moe_w_up, moe_w_down, final_g)
    y_sample = trunk(x_sample, norm1_g, w_in, conv_w, conv_b, w_fourier, w_conv_out, w_o, norm2_g,
                     dense_w_gate, dense_w_up, dense_w_down, router_w, moe_w_gate, moe_w_up, moe_w_down, final_g)
    return (y_prompt, y_sample)
```

```python
import functools
import math

import jax
import jax.numpy as jnp
from jax import lax
from jax.experimental import pallas as pl
from jax.experimental.pallas import tpu as pltpu

F_GROUPS = 8
TOP_K = 2
EPS = 1e-6
BF16 = jnp.bfloat16
F32 = jnp.float32

V7X_LANES = 128
V7X_SUBLANES = 8
V7X_VMEM_BYTES = 64 * 1024 * 1024
VMEM_LIMIT = V7X_VMEM_BYTES - 8 * 1024 * 1024

TOKEN_TILE = 512
MIXER_TILE = 256
INPROJ_COLS = 1024
FFN_COLS = 512
EXPERT_TILE = 512
DFT_RADIX = 128


def _params(*sem):
    return pltpu.CompilerParams(dimension_semantics=sem, vmem_limit_bytes=VMEM_LIMIT)


def _const_spec(shape):
    return pl.BlockSpec(shape, lambda *_: (0,) * len(shape), pipeline_mode=pl.Buffered(1))


def _rmsnorm(x, g):
    return x * lax.rsqrt(jnp.mean(x * x, axis=-1, keepdims=True) + EPS) * g


def _norm_inproj_kernel(x_ref, g_ref, w_ref, zf_ref, zr_ref, h_ref):
    j = pl.program_id(1)

    @pl.when(j == 0)
    def _():
        h_ref[...] = _rmsnorm(x_ref[...], g_ref[...]).astype(BF16)
        zf_ref[...] = jnp.dot(h_ref[...], w_ref[...], preferred_element_type=F32)

    @pl.when(j > 0)
    def _():
        zr_ref[...] = jnp.dot(h_ref[...], w_ref[...], preferred_element_type=F32).astype(BF16)


def _norm_inproj(x, g, w, w_f):
    t, d = x.shape
    n_in = w.shape[1]
    tm, tn = TOKEN_TILE, INPROJ_COLS
    assert w_f == tn and t % tm == 0 and (n_in - w_f) % tn == 0
    return pl.pallas_call(
        _norm_inproj_kernel,
        grid=(t // tm, n_in // tn),
        in_specs=[
            pl.BlockSpec((tm, d), lambda i, j: (i, 0)),
            pl.BlockSpec((1, d), lambda i, j: (0, 0)),
            pl.BlockSpec((d, tn), lambda i, j: (0, j)),
        ],
        out_specs=[
            pl.BlockSpec((tm, tn), lambda i, j: (i, 0)),
            pl.BlockSpec((tm, tn), lambda i, j: (i, jnp.maximum(j - 1, 0))),
        ],
        out_shape=[
            jax.ShapeDtypeStruct((t, w_f), F32),
            jax.ShapeDtypeStruct((t, n_in - w_f), BF16),
        ],
        scratch_shapes=[pltpu.VMEM((tm, d), BF16)],
        compiler_params=_params("parallel", "arbitrary"),
        name="norm_inproj",
    )(x, g.reshape(1, d), w)


def _dft_tables(n):
    j = lax.broadcasted_iota(jnp.int32, (n, n), 0)
    k = lax.broadcasted_iota(jnp.int32, (n, n), 1)
    ang = ((j * k) % n).astype(F32) * (2.0 * math.pi / n)
    return jnp.cos(ang), jnp.sin(ang)


def _dft_direct_kernel(x_ref, csc_ref, css_ref, o_ref, ab_ref, *, gd):
    s = x_ref.shape[0]

    @pl.when(pl.program_id(1) == 0)
    def _():
        for g in range(x_ref.shape[1] // gd):
            cols = slice(g * gd, (g + 1) * gd)
            ab = jnp.dot(x_ref[:, cols].astype(BF16), csc_ref[...], preferred_element_type=F32)
            ab_ref[0:s, cols] = ab[:, :gd].astype(BF16)
            ab_ref[s:2 * s, cols] = ab[:, gd:].astype(BF16)

    o_ref[...] = jnp.dot(css_ref[...], ab_ref[...], preferred_element_type=F32)


def _dft_direct(zf, nseq, s, gd, csc):
    w_f = zf.shape[1]
    tm = TOKEN_TILE
    cs, sn = _dft_tables(s)
    scale = 1.0 / math.sqrt(s * gd)
    css = (jnp.concatenate([cs, -sn], axis=1) * scale).astype(BF16)
    return pl.pallas_call(
        functools.partial(_dft_direct_kernel, gd=gd),
        grid=(nseq, s // tm),
        in_specs=[
            pl.BlockSpec((s, w_f), lambda b, i: (b, 0)),
            _const_spec((gd, 2 * gd)),
            pl.BlockSpec((tm, 2 * s), lambda b, i: (i, 0)),
        ],
        out_specs=pl.BlockSpec((tm, w_f), lambda b, i: (b * (s // tm) + i, 0)),
        out_shape=jax.ShapeDtypeStruct((nseq * s, w_f), F32),
        scratch_shapes=[pltpu.VMEM((2 * s, w_f), BF16)],
        compiler_params=_params("parallel", "arbitrary"),
        name="dft_direct",
    )(zf, csc, css)


def _dft_stage1_kernel(x_ref, csc_ref, m1_ref, twc_ref, tws_ref, tr_ref, ti_ref,
                       x2, tr2, ti2, xs_ref, ab_ref, *, n1, gd):
    nf = x_ref.shape[1]
    x2[...] = x_ref[...].reshape(n1 * nf, gd)
    for f in range(nf):
        xs_ref[f * n1:(f + 1) * n1, :] = x2[pl.ds(f, n1, stride=nf), :].astype(BF16)
    ab = jnp.dot(xs_ref[...], csc_ref[...], preferred_element_type=F32)
    for f in range(nf):
        ab_ref[0:n1, f * gd:(f + 1) * gd] = ab[f * n1:(f + 1) * n1, :gd].astype(BF16)
        ab_ref[n1:2 * n1, f * gd:(f + 1) * gd] = ab[f * n1:(f + 1) * n1, gd:].astype(BF16)
    gq = jnp.dot(m1_ref[...], ab_ref[...], preferred_element_type=F32)
    for f in range(nf):
        gr = gq[0:n1, f * gd:(f + 1) * gd]
        gi = gq[n1:2 * n1, f * gd:(f + 1) * gd]
        tc = twc_ref[f]
        ts = tws_ref[f]
        tr2[pl.ds(f, n1, stride=nf), :] = gr * tc + gi * ts
        ti2[pl.ds(f, n1, stride=nf), :] = gi * tc - gr * ts
    tr_ref[...] = tr2[...].reshape(n1, nf, gd)
    ti_ref[...] = ti2[...].reshape(n1, nf, gd)


def _dft_stage2_kernel(tr_ref, ti_ref, c2s2_ref, o_ref, o2, tcat_ref, *, n2, gd):
    nk = tr_ref.shape[0]
    for q in range(nk):
        tcat_ref[0:n2, q * gd:(q + 1) * gd] = tr_ref[q].astype(BF16)
        tcat_ref[n2:2 * n2, q * gd:(q + 1) * gd] = ti_ref[q].astype(BF16)
    y = jnp.dot(c2s2_ref[...], tcat_ref[...], preferred_element_type=F32)
    for q in range(nk):
        o2[pl.ds(q, n2, stride=nk), :] = y[:, q * gd:(q + 1) * gd]
    o_ref[...] = o2[...].reshape(n2, nk, gd)


def _dft_two_stage(zf, row0, nseq, s, gd, csc):
    w_f = zf.shape[1]
    n1 = DFT_RADIX
    n2 = s // n1
    nf = V7X_SUBLANES
    assert s % n1 == 0 and n2 % nf == 0 and n1 % nf == 0 and gd == V7X_LANES
    if row0 % s == 0:
        blk0 = row0 // s
    else:
        zf, blk0 = zf[row0:], 0
    ng = w_f // gd
    scale = 1.0 / math.sqrt(s * gd)

    c1, s1 = _dft_tables(n1)
    m1 = jnp.concatenate([jnp.concatenate([c1, -s1], axis=1),
                          jnp.concatenate([-s1, -c1], axis=1)], axis=0).astype(BF16)
    a = lax.broadcasted_iota(jnp.int32, (n2, n1), 0)
    b = lax.broadcasted_iota(jnp.int32, (n2, n1), 1)
    ang = ((a * b) % s).astype(F32) * (2.0 * math.pi / s)
    twc = jnp.broadcast_to((jnp.cos(ang) * scale)[:, :, None], (n2, n1, gd))
    tws = jnp.broadcast_to((jnp.sin(ang) * scale)[:, :, None], (n2, n1, gd))
    c2, s2 = _dft_tables(n2)
    c2s2 = jnp.concatenate([c2, s2], axis=1).astype(BF16)

    x3 = zf.reshape(zf.shape[0] // n2, n2, w_f)
    t_shape = jax.ShapeDtypeStruct((nseq * n1, n2, w_f), F32)
    tr, ti = pl.pallas_call(
        functools.partial(_dft_stage1_kernel, n1=n1, gd=gd),
        grid=(nseq, n2 // nf, ng),
        in_specs=[
            pl.BlockSpec((n1, nf, gd), lambda b, f, g: (blk0 + b, f, g)),
            _const_spec((gd, 2 * gd)),
            _const_spec((2 * n1, 2 * n1)),
            pl.BlockSpec((nf, n1, gd), lambda b, f, g: (f, 0, 0)),
            pl.BlockSpec((nf, n1, gd), lambda b, f, g: (f, 0, 0)),
        ],
        out_specs=[
            pl.BlockSpec((n1, nf, gd), lambda b, f, g: (b, f, g)),
            pl.BlockSpec((n1, nf, gd), lambda b, f, g: (b, f, g)),
        ],
        out_shape=[t_shape, t_shape],
        scratch_shapes=[pltpu.VMEM((n1 * nf, gd), F32), pltpu.VMEM((n1 * nf, gd), F32),
                        pltpu.VMEM((n1 * nf, gd), F32),
                        pltpu.VMEM((nf * n1, gd), BF16), pltpu.VMEM((2 * n1, nf * gd), BF16)],
        compiler_params=_params("parallel", "parallel", "arbitrary"),
        name="dft_stage1",
    )(x3, csc, m1, twc, tws)

    y = pl.pallas_call(
        functools.partial(_dft_stage2_kernel, n2=n2, gd=gd),
        grid=(nseq, n1 // nf, ng),
        in_specs=[
            pl.BlockSpec((nf, n2, gd), lambda b, q, g: (b * (n1 // nf) + q, 0, g)),
            pl.BlockSpec((nf, n2, gd), lambda b, q, g: (b * (n1 // nf) + q, 0, g)),
            _const_spec((n2, 2 * n2)),
        ],
        out_specs=pl.BlockSpec((n2, nf, gd), lambda b, q, g: (b, q, g)),
        out_shape=jax.ShapeDtypeStruct((nseq * n2, n1, w_f), F32),
        scratch_shapes=[pltpu.VMEM((n2 * nf, gd), F32), pltpu.VMEM((2 * n2, nf * gd), BF16)],
        compiler_params=_params("parallel", "parallel", "arbitrary"),
        name="dft_stage2",
    )(tr, ti, c2s2)
    return y.reshape(nseq * s, w_f)


def _mixer_out_kernel(edge_ref, yfa_ref, yfb_ref, zr_ref, pc_ref, pv_ref, nc_ref, nv_ref, x_ref,
                      cw_ref, cb_ref, wf_ref, wc_ref, wo_ref, o_ref, *, n_a, w_c, d):
    i = pl.program_id(0)
    tm = x_ref.shape[0]
    halo = pc_ref.shape[0]
    zb = zr_ref[:, 0:w_c].astype(F32)
    u = zr_ref[:, w_c:2 * w_c].astype(F32) * zr_ref[:, 2 * w_c:3 * w_c].astype(F32)
    u_prev = pc_ref[halo - 1:halo, :].astype(F32) * pv_ref[halo - 1:halo, :].astype(F32)
    u_next = nc_ref[0:1, :].astype(F32) * nv_ref[0:1, :].astype(F32)
    u_prev = jnp.where(edge_ref[i, 0] > 0, u_prev, 0.0)
    u_next = jnp.where(edge_ref[i, 1] > 0, u_next, 0.0)
    row = lax.broadcasted_iota(jnp.int32, (tm, w_c), 0)
    u_m1 = jnp.where(row == 0, u_prev, pltpu.roll(u, 1, 0))
    u_p1 = jnp.where(row == tm - 1, u_next, pltpu.roll(u, tm - 1, 0))
    cv = u_m1 * cw_ref[0:1, :] + u * cw_ref[1:2, :] + u_p1 * cw_ref[2:3, :] + cb_ref[...]
    yc = (zb * cv).astype(BF16)
    yf = jnp.where(i < n_a, yfa_ref[...], yfb_ref[...]).astype(BF16)
    gf = jax.nn.sigmoid(zr_ref[:, 3 * w_c:3 * w_c + d].astype(F32))
    gc = jax.nn.sigmoid(zr_ref[:, 3 * w_c + d:3 * w_c + 2 * d].astype(F32))
    m = (gf * jnp.dot(yf, wf_ref[...], preferred_element_type=F32)
         + gc * jnp.dot(yc, wc_ref[...], preferred_element_type=F32))
    o_ref[...] = x_ref[...] + jnp.dot(m.astype(BF16), wo_ref[...], preferred_element_type=F32)


def _mixer_out(seq_bounds, yf_a, yf_b, zr, x, conv_w, conv_b, w_fourier, w_conv_out, w_o):
    t, d = x.shape
    w_f = w_fourier.shape[0]
    w_c = w_conv_out.shape[0]
    tm = MIXER_TILE
    halo = 2 * V7X_SUBLANES
    n_a = yf_a.shape[0] // tm
    n_b = yf_b.shape[0] // tm
    nh = t // halo
    hb = tm // halo
    assert w_c % V7X_LANES == 0 and t % tm == 0 and all(b % tm == 0 for b in seq_bounds)
    edges = jnp.asarray([[0 if i * tm in seq_bounds else 1, 0 if (i + 1) * tm in seq_bounds else 1]
                         for i in range(t // tm)], dtype=jnp.int32)
    return pl.pallas_call(
        functools.partial(_mixer_out_kernel, n_a=n_a, w_c=w_c, d=d),
        grid_spec=pltpu.PrefetchScalarGridSpec(
            num_scalar_prefetch=1,
            grid=(t // tm,),
            in_specs=[
                pl.BlockSpec((tm, w_f), lambda i, e: (jnp.minimum(i, n_a - 1), 0)),
                pl.BlockSpec((tm, w_f), lambda i, e: (jnp.clip(i - n_a, 0, n_b - 1), 0)),
                pl.BlockSpec((tm, zr.shape[1]), lambda i, e: (i, 0)),
                pl.BlockSpec((halo, w_c), lambda i, e: (jnp.maximum(i * hb - 1, 0), 1)),
                pl.BlockSpec((halo, w_c), lambda i, e: (jnp.maximum(i * hb - 1, 0), 2)),
                pl.BlockSpec((halo, w_c), lambda i, e: (jnp.minimum((i + 1) * hb, nh - 1), 1)),
                pl.BlockSpec((halo, w_c), lambda i, e: (jnp.minimum((i + 1) * hb, nh - 1), 2)),
                pl.BlockSpec((tm, d), lambda i, e: (i, 0)),
                _const_spec((conv_w.shape[0], w_c)),
                _const_spec((1, w_c)),
                _const_spec((w_f, d)),
                _const_spec((w_c, d)),
                _const_spec((d, d)),
            ],
            out_specs=pl.BlockSpec((tm, d), lambda i, e: (i, 0)),
        ),
        out_shape=jax.ShapeDtypeStruct((t, d), F32),
        compiler_params=_params("parallel"),
        name="mixer_out",
    )(edges, yf_a, yf_b, zr, zr, zr, zr, zr, x, conv_w, conv_b.reshape(1, w_c),
      w_fourier, w_conv_out, w_o)


def _dense_ffn_kernel(x_ref, g_ref, wg_ref, wu_ref, wd_ref, fg_ref, o_ref, h_ref, *, final_norm):
    j = pl.program_id(1)

    @pl.when(j == 0)
    def _():
        h_ref[...] = _rmsnorm(x_ref[...], g_ref[...]).astype(BF16)
        o_ref[...] = x_ref[...]

    h = h_ref[...]
    a = jnp.dot(h, wg_ref[...], preferred_element_type=F32)
    b = jnp.dot(h, wu_ref[...], preferred_element_type=F32)
    act = (a * jax.nn.sigmoid(a) * b).astype(BF16)
    o_ref[...] += jnp.dot(act, wd_ref[...], preferred_element_type=F32)

    if final_norm:
        @pl.when(j == pl.num_programs(1) - 1)
        def _():
            o_ref[...] = _rmsnorm(o_ref[...], fg_ref[...])


def _dense_ffn(x, g, wg, wu, wd, final_g):
    t, d = x.shape
    f = wg.shape[1]
    tm, tf = TOKEN_TILE, FFN_COLS
    assert t % tm == 0 and f % tf == 0
    final_norm = final_g is not None
    fg = (final_g if final_norm else jnp.ones((d,), F32)).reshape(1, d)
    return pl.pallas_call(
        functools.partial(_dense_ffn_kernel, final_norm=final_norm),
        grid=(t // tm, f // tf),
        in_specs=[
            pl.BlockSpec((tm, d), lambda i, j: (i, 0)),
            pl.BlockSpec((1, d), lambda i, j: (0, 0)),
            pl.BlockSpec((d, tf), lambda i, j: (0, j)),
            pl.BlockSpec((d, tf), lambda i, j: (0, j)),
            pl.BlockSpec((tf, d), lambda i, j: (j, 0)),
            pl.BlockSpec((1, d), lambda i, j: (0, 0)),
        ],
        out_specs=pl.BlockSpec((tm, d), lambda i, j: (i, 0)),
        out_shape=jax.ShapeDtypeStruct((t, d), F32),
        scratch_shapes=[pltpu.VMEM((tm, d), BF16)],
        compiler_params=_params("parallel", "arbitrary"),
        name="dense_ffn",
    )(x, g.reshape(1, d), wg, wu, wd, fg)


def _router_kernel(x_ref, g_ref, rw_ref, tri_ref, ri_ref, rp_ref, cnt_ref, carry_ref, *, n_exp):
    @pl.when(pl.program_id(0) == 0)
    def _():
        carry_ref[...] = jnp.zeros_like(carry_ref)

    tm = x_ref.shape[0]
    lanes = rw_ref.shape[1]
    h = _rmsnorm(x_ref[...], g_ref[...])
    logits = jnp.dot(h, rw_ref[...], preferred_element_type=F32, precision=lax.Precision.HIGHEST)
    lane = lax.broadcasted_iota(jnp.int32, (tm, lanes), 1).astype(F32)
    neg = jnp.float32(-jnp.inf)
    lg = jnp.where(lane < n_exp, logits, neg)
    m1 = jnp.max(lg, axis=1, keepdims=True)
    i1 = jnp.min(jnp.where(lg == m1, lane, float(lanes)), axis=1, keepdims=True)
    lg2 = jnp.where(lane == i1, neg, lg)
    m2 = jnp.max(lg2, axis=1, keepdims=True)
    i2 = jnp.min(jnp.where(lg2 == m2, lane, float(lanes)), axis=1, keepdims=True)
    sel = jnp.where((lane == i1) | (lane == i2), 1.0, 0.0)
    before = jnp.dot(tri_ref[...], sel.astype(BF16), preferred_element_type=F32)
    rank = carry_ref[...] + before
    r1 = jnp.sum(jnp.where(lane == i1, rank, 0.0), axis=1, keepdims=True)
    r2 = jnp.sum(jnp.where(lane == i2, rank, 0.0), axis=1, keepdims=True)
    carry_ref[...] += jnp.sum(sel, axis=0, keepdims=True)
    cnt_ref[...] = carry_ref[...]
    e21 = jnp.exp(m2 - m1)
    p1 = 1.0 / (1.0 + e21)
    p2 = e21 / (1.0 + e21)
    rp_ref[...] = jnp.where(lane == 0, p1, jnp.where(lane == 1, p2, 0.0))
    q = jnp.where(lane == 0, i1, jnp.where(lane == 1, i2, jnp.where(lane == 2, r1, jnp.where(lane == 3, r2, 0.0))))
    ri_ref[...] = q.T[0:ri_ref.shape[0], :].astype(jnp.int32)


def _router(x, g, router_w):
    t, d = x.shape
    n_exp = router_w.shape[1]
    tm = TOKEN_TILE
    lanes = V7X_LANES
    rw = jnp.zeros((d, lanes), F32).at[:, :n_exp].set(router_w)
    tri = (lax.broadcasted_iota(jnp.int32, (tm, tm), 1)
           < lax.broadcasted_iota(jnp.int32, (tm, tm), 0)).astype(BF16)
    return pl.pallas_call(
        functools.partial(_router_kernel, n_exp=n_exp),
        grid=(t // tm,),
        in_specs=[
            pl.BlockSpec((tm, d), lambda i: (i, 0)),
            _const_spec((1, d)),
            _const_spec((d, lanes)),
            _const_spec((tm, tm)),
        ],
        out_specs=[
            pl.BlockSpec((V7X_SUBLANES, tm), lambda i: (0, i)),
            pl.BlockSpec((tm, lanes), lambda i: (i, 0)),
            pl.BlockSpec((1, lanes), lambda i: (0, 0)),
        ],
        out_shape=[
            jax.ShapeDtypeStruct((V7X_SUBLANES, t), jnp.int32),
            jax.ShapeDtypeStruct((t, lanes), F32),
            jax.ShapeDtypeStruct((1, lanes), F32),
        ],
        scratch_shapes=[pltpu.VMEM((1, lanes), F32)],
        compiler_params=_params("arbitrary"),
        name="router",
    )(x, g.reshape(1, d), rw, tri)


def _row_copy(src_ref, src_row, dst_ref, dst_row, sem):
    return pltpu.make_async_copy(src_ref.at[pl.ds(src_row, 1), :], dst_ref.at[pl.ds(dst_row, 1), :], sem)


def _dispatch_kernel(off_ref, x_ref, g_ref, ri_ref, hs_in_ref, hs_ref, h_ref, sem):
    del hs_in_ref
    tm = x_ref.shape[0]
    h_ref[...] = _rmsnorm(x_ref[...], g_ref[...])

    def issue(s, c):
        for k in range(TOP_K):
            dst = off_ref[ri_ref[k, s]] + ri_ref[TOP_K + k, s]
            _row_copy(h_ref, s, hs_ref, dst, sem).start()
        return c

    lax.fori_loop(0, tm, issue, 0)

    def drain(s, c):
        _row_copy(h_ref, 0, hs_ref, 0, sem).wait()
        return c

    lax.fori_loop(0, TOP_K * tm, drain, 0)


def _dispatch(x, g, route_i, offsets, n_rows):
    t, d = x.shape
    tm = TOKEN_TILE
    hs0 = jnp.zeros((n_rows, d), F32)
    return pl.pallas_call(
        _dispatch_kernel,
        grid_spec=pltpu.PrefetchScalarGridSpec(
            num_scalar_prefetch=1,
            grid=(t // tm,),
            in_specs=[
                pl.BlockSpec((tm, d), lambda i, o: (i, 0)),
                _const_spec((1, d)),
                pl.BlockSpec((route_i.shape[0], tm), lambda i, o: (0, i), memory_space=pltpu.SMEM),
                pl.BlockSpec(memory_space=pl.ANY),
            ],
            out_specs=pl.BlockSpec(memory_space=pl.ANY),
            scratch_shapes=[pltpu.VMEM((tm, d), F32), pltpu.SemaphoreType.DMA(())],
        ),
        out_shape=jax.ShapeDtypeStruct((n_rows, d), F32),
        input_output_aliases={4: 0},
        compiler_params=_params("arbitrary"),
        name="dispatch",
    )(offsets, x, g.reshape(1, d), route_i, hs0)


def _expert_ffn_kernel(te_ref, nv_ref, hs_ref, wg_ref, wu_ref, wd_ref, o_ref, xb_ref):
    del te_ref
    j = pl.program_id(1)
    used = pl.program_id(0) < nv_ref[0]

    @pl.when(jnp.logical_and(jnp.logical_not(used), j == 0))
    def _():
        o_ref[...] = jnp.zeros_like(o_ref)

    @pl.when(used)
    def _():
        @pl.when(j == 0)
        def _():
            xb_ref[...] = hs_ref[...].astype(BF16)

        xb = xb_ref[...]
        a = jnp.dot(xb, wg_ref[...], preferred_element_type=F32)
        b = jnp.dot(xb, wu_ref[...], preferred_element_type=F32)
        act = (a * jax.nn.sigmoid(a) * b).astype(BF16)
        y = jnp.dot(act, wd_ref[...], preferred_element_type=F32)

        @pl.when(j == 0)
        def _():
            o_ref[...] = y

        @pl.when(j > 0)
        def _():
            o_ref[...] += y


def _expert_ffn(hs, tile_expert, n_valid, wg, wu, wd):
    n_rows, d = hs.shape
    f = wg.shape[2]
    tm, tf = EXPERT_TILE, FFN_COLS
    assert n_rows % tm == 0 and f % tf == 0
    nj = f // tf

    def col(g, j, nv):
        return jnp.where(g < nv[0], j, nj - 1)

    return pl.pallas_call(
        _expert_ffn_kernel,
        grid_spec=pltpu.PrefetchScalarGridSpec(
            num_scalar_prefetch=2,
            grid=(n_rows // tm, nj),
            in_specs=[
                pl.BlockSpec((tm, d), lambda g, j, te, nv: (g, 0)),
                pl.BlockSpec((None, d, tf), lambda g, j, te, nv: (te[g], 0, col(g, j, nv))),
                pl.BlockSpec((None, d, tf), lambda g, j, te, nv: (te[g], 0, col(g, j, nv))),
                pl.BlockSpec((None, tf, d), lambda g, j, te, nv: (te[g], col(g, j, nv), 0)),
            ],
            out_specs=pl.BlockSpec((tm, d), lambda g, j, te, nv: (g, 0)),
            scratch_shapes=[pltpu.VMEM((tm, d), BF16)],
        ),
        out_shape=jax.ShapeDtypeStruct((n_rows, d), F32),
        compiler_params=_params("arbitrary", "arbitrary"),
        name="expert_ffn",
    )(tile_expert, n_valid, hs, wg, wu, wd)


def _combine_kernel(off_ref, x_ref, ri_ref, rp_ref, fg_ref, ys_ref, o_ref, y_ref, sem, *, final_norm):
    tm = x_ref.shape[0]

    def issue(s, c):
        for k in range(TOP_K):
            src = off_ref[ri_ref[k, s]] + ri_ref[TOP_K + k, s]
            _row_copy(ys_ref, src, y_ref.at[k], s, sem).start()
        return c

    lax.fori_loop(0, tm, issue, 0)

    def drain(s, c):
        _row_copy(ys_ref, 0, y_ref.at[0], 0, sem).wait()
        return c

    lax.fori_loop(0, TOP_K * tm, drain, 0)

    y = x_ref[...] + rp_ref[:, 0:1] * y_ref[0] + rp_ref[:, 1:2] * y_ref[1]
    if final_norm:
        y = _rmsnorm(y, fg_ref[...])
    o_ref[...] = y


def _combine(x, route_i, route_p, offsets, ys, final_g, row0, rows):
    t, d = x.shape
    tm = TOKEN_TILE
    assert row0 % tm == 0 and rows % tm == 0
    b0 = row0 // tm
    final_norm = final_g is not None
    fg = (final_g if final_norm else jnp.ones((d,), F32)).reshape(1, d)
    return pl.pallas_call(
        functools.partial(_combine_kernel, final_norm=final_norm),
        grid_spec=pltpu.PrefetchScalarGridSpec(
            num_scalar_prefetch=1,
            grid=(rows // tm,),
            in_specs=[
                pl.BlockSpec((tm, d), lambda i, o: (b0 + i, 0)),
                pl.BlockSpec((route_i.shape[0], tm), lambda i, o: (0, b0 + i), memory_space=pltpu.SMEM),
                pl.BlockSpec((tm, route_p.shape[1]), lambda i, o: (b0 + i, 0)),
                _const_spec((1, d)),
                pl.BlockSpec(memory_space=pl.ANY),
            ],
            out_specs=pl.BlockSpec((tm, d), lambda i, o: (i, 0)),
            scratch_shapes=[pltpu.VMEM((TOP_K, tm, d), F32), pltpu.SemaphoreType.DMA(())],
        ),
        out_shape=jax.ShapeDtypeStruct((rows, d), F32),
        compiler_params=_params("arbitrary"),
        name="combine",
    )(offsets, x, route_i, route_p, fg, ys)


def _moe_plan(counts, n_exp, n_tiles):
    tm = EXPERT_TILE
    cnt = counts[0, :n_exp].astype(jnp.int32)
    tiles = (cnt + tm - 1) // tm
    tile_end = jnp.cumsum(tiles)
    offsets = (tile_end - tiles) * tm
    n_valid = tile_end[-1]
    g = jnp.arange(n_tiles, dtype=jnp.int32)
    gc = jnp.minimum(g, n_valid - 1)
    tile_expert = jnp.sum((gc[:, None] >= tile_end[None, :]).astype(jnp.int32), axis=1)
    return offsets, tile_expert, n_valid.reshape(1)


def _moe_layer(x, g, router_w, wg, wu, wd, final_g, splits):
    t, d = x.shape
    n_exp = router_w.shape[1]
    n_tiles = (t * TOP_K) // EXPERT_TILE + n_exp
    route_i, route_p, counts = _router(x, g, router_w)
    offsets, tile_expert, n_valid = _moe_plan(counts, n_exp, n_tiles)
    hs = _dispatch(x, g, route_i, offsets, n_tiles * EXPERT_TILE)
    ys = _expert_ffn(hs, tile_expert, n_valid, wg, wu, wd)
    return [_combine(x, route_i, route_p, offsets, ys, final_g, r0, n) for r0, n in splits]


def kernel(x_prompt, x_sample, norm1_g, w_in, conv_w, conv_b, w_fourier, w_conv_out, w_o, norm2_g,
           dense_w_gate, dense_w_up, dense_w_down, router_w, moe_w_gate, moe_w_up, moe_w_down, final_g):
    bp, sp, d = x_prompt.shape
    bs, ss, _ = x_sample.shape
    tp, ts = bp * sp, bs * ss
    t = tp + ts
    depth = norm1_g.shape[0]
    w_f = w_fourier.shape[1]
    gd = w_f // F_GROUPS
    tm = TOKEN_TILE
    assert sp % tm == 0 and ss % tm == 0

    x = jnp.concatenate([x_prompt.reshape(tp, d), x_sample.reshape(ts, d)], axis=0)

    seq_bounds = frozenset([b * sp for b in range(bp + 1)] + [tp + b * ss for b in range(bs + 1)])

    cc, sc = _dft_tables(gd)
    csc = jnp.concatenate([cc, sc], axis=1).astype(BF16)

    for l in range(depth):
        zf, zr = _norm_inproj(x, norm1_g[l], w_in[l].astype(BF16), w_f)
        yf_p = _dft_direct(zf, bp, sp, gd, csc)
        yf_s = _dft_two_stage(zf, tp, bs, ss, gd, csc)
        x = _mixer_out(seq_bounds, yf_p, yf_s, zr, x, conv_w[l], conv_b[l], w_fourier[l].astype(BF16),
                       w_conv_out[l].astype(BF16), w_o[l].astype(BF16))
        last = l == depth - 1
        fg = final_g if last else None
        j = l // 2
        if l % 2 == 0:
            x = _dense_ffn(x, norm2_g[l], dense_w_gate[j].astype(BF16), dense_w_up[j].astype(BF16),
                           dense_w_down[j].astype(BF16), fg)
            if last:
                outs = [x[:tp], x[tp:]]
        else:
            splits = [(0, tp), (tp, ts)] if last else [(0, t)]
            outs = _moe_layer(x, norm2_g[l], router_w[j], moe_w_gate[j].astype(BF16),
                              moe_w_up[j].astype(BF16), moe_w_down[j].astype(BF16), fg, splits)
            if not last:
                x = outs[0]
    return outs[0].reshape(bp, sp, d), outs[1].reshape(bs, ss, d)
```

```python
import functools
import math

import jax
import jax.numpy as jnp
from jax import lax
from jax.experimental import pallas as pl
from jax.experimental.pallas import tpu as pltpu

F_GROUPS = 8
TOP_K = 2
EPS = 1e-6
BF16 = jnp.bfloat16
F32 = jnp.float32

V7X_LANES = 128
V7X_SUBLANES = 8
V7X_VMEM_BYTES = 64 * 1024 * 1024
VMEM_LIMIT = V7X_VMEM_BYTES - 8 * 1024 * 1024

TOKEN_TILE = 512
MIXER_TILE = 256
INPROJ_TILE = 1024
INPROJ_COLS = 1024
FFN_COLS = 512
EXPERT_TILE = 512
EXPERT_COLS = 1024
DMA_ISSUE_UNROLL = 8
DFT_RADIX = 128


def _params(*sem):
    return pltpu.CompilerParams(dimension_semantics=sem, vmem_limit_bytes=VMEM_LIMIT)


def _const_spec(shape):
    return pl.BlockSpec(shape, lambda *_: (0,) * len(shape), pipeline_mode=pl.Buffered(1))


def _rmsnorm(x, g):
    return x * lax.rsqrt(jnp.mean(x * x, axis=-1, keepdims=True) + EPS) * g


def _norm_inproj_kernel(x_ref, g_ref, w_ref, zf_ref, zr_ref, h_ref):
    j = pl.program_id(1)

    @pl.when(j == 0)
    def _():
        h_ref[...] = _rmsnorm(x_ref[...], g_ref[...]).astype(BF16)
        zf_ref[...] = jnp.dot(h_ref[...], w_ref[...], preferred_element_type=F32)

    @pl.when(j > 0)
    def _():
        zr_ref[...] = jnp.dot(h_ref[...], w_ref[...], preferred_element_type=F32).astype(BF16)


def _norm_inproj(x, g, w, w_f):
    t, d = x.shape
    n_in = w.shape[1]
    tm, tn = INPROJ_TILE, INPROJ_COLS
    assert w_f == tn and t % tm == 0 and (n_in - w_f) % tn == 0
    return pl.pallas_call(
        _norm_inproj_kernel,
        grid=(t // tm, n_in // tn),
        in_specs=[
            pl.BlockSpec((tm, d), lambda i, j: (i, 0)),
            pl.BlockSpec((1, d), lambda i, j: (0, 0)),
            pl.BlockSpec((d, tn), lambda i, j: (0, j)),
        ],
        out_specs=[
            pl.BlockSpec((tm, tn), lambda i, j: (i, 0)),
            pl.BlockSpec((tm, tn), lambda i, j: (i, jnp.maximum(j - 1, 0))),
        ],
        out_shape=[
            jax.ShapeDtypeStruct((t, w_f), F32),
            jax.ShapeDtypeStruct((t, n_in - w_f), BF16),
        ],
        scratch_shapes=[pltpu.VMEM((tm, d), BF16)],
        compiler_params=_params("parallel", "arbitrary"),
        name="norm_inproj",
    )(x, g.reshape(1, d), w)


def _dft_tables(n):
    j = lax.broadcasted_iota(jnp.int32, (n, n), 0)
    k = lax.broadcasted_iota(jnp.int32, (n, n), 1)
    ang = ((j * k) % n).astype(F32) * (2.0 * math.pi / n)
    return jnp.cos(ang), jnp.sin(ang)


def _dft_direct_kernel(x_ref, csc_ref, css_ref, o_ref, ab_ref, *, gd):
    s = x_ref.shape[0]

    @pl.when(pl.program_id(1) == 0)
    def _():
        for g in range(x_ref.shape[1] // gd):
            cols = slice(g * gd, (g + 1) * gd)
            ab = jnp.dot(x_ref[:, cols].astype(BF16), csc_ref[...], preferred_element_type=F32)
            ab_ref[0:s, cols] = ab[:, :gd].astype(BF16)
            ab_ref[s:2 * s, cols] = ab[:, gd:].astype(BF16)

    o_ref[...] = jnp.dot(css_ref[...], ab_ref[...], preferred_element_type=F32)


def _dft_direct(zf, nseq, s, gd, csc):
    w_f = zf.shape[1]
    tm = TOKEN_TILE
    cs, sn = _dft_tables(s)
    scale = 1.0 / math.sqrt(s * gd)
    css = (jnp.concatenate([cs, -sn], axis=1) * scale).astype(BF16)
    return pl.pallas_call(
        functools.partial(_dft_direct_kernel, gd=gd),
        grid=(nseq, s // tm),
        in_specs=[
            pl.BlockSpec((s, w_f), lambda b, i: (b, 0)),
            _const_spec((gd, 2 * gd)),
            pl.BlockSpec((tm, 2 * s), lambda b, i: (i, 0)),
        ],
        out_specs=pl.BlockSpec((tm, w_f), lambda b, i: (b * (s // tm) + i, 0)),
        out_shape=jax.ShapeDtypeStruct((nseq * s, w_f), F32),
        scratch_shapes=[pltpu.VMEM((2 * s, w_f), BF16)],
        compiler_params=_params("parallel", "arbitrary"),
        name="dft_direct",
    )(zf, csc, css)


def _dft_stage1_kernel(x_ref, csc_ref, m1_ref, twc_ref, tws_ref, tr_ref, ti_ref,
                       x2, tr2, ti2, xs_ref, ab_ref, *, n1, gd):
    nf = x_ref.shape[1]
    for g in range(x_ref.shape[2] // gd):
        cols = slice(g * gd, (g + 1) * gd)
        x2[...] = x_ref[:, :, cols].reshape(n1 * nf, gd)
        for f in range(nf):
            xs_ref[f * n1:(f + 1) * n1, :] = x2[pl.ds(f, n1, stride=nf), :].astype(BF16)
        ab = jnp.dot(xs_ref[...], csc_ref[...], preferred_element_type=F32)
        for f in range(nf):
            ab_ref[0:n1, f * gd:(f + 1) * gd] = ab[f * n1:(f + 1) * n1, :gd].astype(BF16)
            ab_ref[n1:2 * n1, f * gd:(f + 1) * gd] = ab[f * n1:(f + 1) * n1, gd:].astype(BF16)
        gq = jnp.dot(m1_ref[...], ab_ref[...], preferred_element_type=F32)
        for f in range(nf):
            gr = gq[0:n1, f * gd:(f + 1) * gd]
            gi = gq[n1:2 * n1, f * gd:(f + 1) * gd]
            tc = twc_ref[f]
            ts = tws_ref[f]
            tr2[pl.ds(f, n1, stride=nf), :] = gr * tc + gi * ts
            ti2[pl.ds(f, n1, stride=nf), :] = gi * tc - gr * ts
        tr_ref[:, :, cols] = tr2[...].reshape(n1, nf, gd)
        ti_ref[:, :, cols] = ti2[...].reshape(n1, nf, gd)


def _dft_stage2_kernel(tr_ref, ti_ref, c2s2_ref, o_ref, o2, tcat_ref, *, n2, gd):
    nk = tr_ref.shape[0]
    for g in range(tr_ref.shape[2] // gd):
        cols = slice(g * gd, (g + 1) * gd)
        for q in range(nk):
            tcat_ref[0:n2, q * gd:(q + 1) * gd] = tr_ref[q, :, cols].astype(BF16)
            tcat_ref[n2:2 * n2, q * gd:(q + 1) * gd] = ti_ref[q, :, cols].astype(BF16)
        y = jnp.dot(c2s2_ref[...], tcat_ref[...], preferred_element_type=F32)
        for q in range(nk):
            o2[pl.ds(q, n2, stride=nk), :] = y[:, q * gd:(q + 1) * gd]
        o_ref[:, :, cols] = o2[...].reshape(n2, nk, gd)


def _dft_two_stage(zf, row0, nseq, s, gd, csc):
    w_f = zf.shape[1]
    n1 = DFT_RADIX
    n2 = s // n1
    nf = V7X_SUBLANES
    assert s % n1 == 0 and n2 % nf == 0 and n1 % nf == 0 and gd == V7X_LANES
    if row0 % s == 0:
        blk0 = row0 // s
    else:
        zf, blk0 = zf[row0:], 0
    scale = 1.0 / math.sqrt(s * gd)

    c1, s1 = _dft_tables(n1)
    m1 = jnp.concatenate([jnp.concatenate([c1, -s1], axis=1),
                          jnp.concatenate([-s1, -c1], axis=1)], axis=0).astype(BF16)
    a = lax.broadcasted_iota(jnp.int32, (n2, n1), 0)
    b = lax.broadcasted_iota(jnp.int32, (n2, n1), 1)
    ang = ((a * b) % s).astype(F32) * (2.0 * math.pi / s)
    twc = jnp.broadcast_to((jnp.cos(ang) * scale)[:, :, None], (n2, n1, gd))
    tws = jnp.broadcast_to((jnp.sin(ang) * scale)[:, :, None], (n2, n1, gd))
    c2, s2 = _dft_tables(n2)
    c2s2 = jnp.concatenate([c2, s2], axis=1).astype(BF16)

    x3 = zf.reshape(zf.shape[0] // n2, n2, w_f)
    t_shape = jax.ShapeDtypeStruct((nseq * n1, n2, w_f), F32)
    tr, ti = pl.pallas_call(
        functools.partial(_dft_stage1_kernel, n1=n1, gd=gd),
        grid=(nseq, n2 // nf),
        in_specs=[
            pl.BlockSpec((n1, nf, w_f), lambda b, f: (blk0 + b, f, 0)),
            _const_spec((gd, 2 * gd)),
            _const_spec((2 * n1, 2 * n1)),
            pl.BlockSpec((nf, n1, gd), lambda b, f: (f, 0, 0)),
            pl.BlockSpec((nf, n1, gd), lambda b, f: (f, 0, 0)),
        ],
        out_specs=[
            pl.BlockSpec((n1, nf, w_f), lambda b, f: (b, f, 0)),
            pl.BlockSpec((n1, nf, w_f), lambda b, f: (b, f, 0)),
        ],
        out_shape=[t_shape, t_shape],
        scratch_shapes=[pltpu.VMEM((n1 * nf, gd), F32), pltpu.VMEM((n1 * nf, gd), F32),
                        pltpu.VMEM((n1 * nf, gd), F32),
                        pltpu.VMEM((nf * n1, gd), BF16), pltpu.VMEM((2 * n1, nf * gd), BF16)],
        compiler_params=_params("parallel", "parallel"),
        name="dft_stage1",
    )(x3, csc, m1, twc, tws)

    y = pl.pallas_call(
        functools.partial(_dft_stage2_kernel, n2=n2, gd=gd),
        grid=(nseq, n1 // nf),
        in_specs=[
            pl.BlockSpec((nf, n2, w_f), lambda b, q: (b * (n1 // nf) + q, 0, 0)),
            pl.BlockSpec((nf, n2, w_f), lambda b, q: (b * (n1 // nf) + q, 0, 0)),
            _const_spec((n2, 2 * n2)),
        ],
        out_specs=pl.BlockSpec((n2, nf, w_f), lambda b, q: (b, q, 0)),
        out_shape=jax.ShapeDtypeStruct((nseq * n2, n1, w_f), F32),
        scratch_shapes=[pltpu.VMEM((n2 * nf, gd), F32), pltpu.VMEM((2 * n2, nf * gd), BF16)],
        compiler_params=_params("parallel", "parallel"),
        name="dft_stage2",
    )(tr, ti, c2s2)
    return y.reshape(nseq * s, w_f)


def _mixer_out_kernel(edge_ref, yfa_ref, yfb_ref, zr_ref, pc_ref, pv_ref, nc_ref, nv_ref, x_ref,
                      cw_ref, cb_ref, wf_ref, wc_ref, wo_ref, o_ref, *, n_a, w_c, d):
    i = pl.program_id(0)
    tm = x_ref.shape[0]
    halo = pc_ref.shape[0]
    zb = zr_ref[:, 0:w_c].astype(F32)
    u = zr_ref[:, w_c:2 * w_c].astype(F32) * zr_ref[:, 2 * w_c:3 * w_c].astype(F32)
    u_prev = pc_ref[halo - 1:halo, :].astype(F32) * pv_ref[halo - 1:halo, :].astype(F32)
    u_next = nc_ref[0:1, :].astype(F32) * nv_ref[0:1, :].astype(F32)
    u_prev = jnp.where(edge_ref[i, 0] > 0, u_prev, 0.0)
    u_next = jnp.where(edge_ref[i, 1] > 0, u_next, 0.0)
    row = lax.broadcasted_iota(jnp.int32, (tm, w_c), 0)
    u_m1 = jnp.where(row == 0, u_prev, pltpu.roll(u, 1, 0))
    u_p1 = jnp.where(row == tm - 1, u_next, pltpu.roll(u, tm - 1, 0))
    cv = u_m1 * cw_ref[0:1, :] + u * cw_ref[1:2, :] + u_p1 * cw_ref[2:3, :] + cb_ref[...]
    yc = (zb * cv).astype(BF16)
    yf = jnp.where(i < n_a, yfa_ref[...], yfb_ref[...]).astype(BF16)
    gf = jax.nn.sigmoid(zr_ref[:, 3 * w_c:3 * w_c + d].astype(F32))
    gc = jax.nn.sigmoid(zr_ref[:, 3 * w_c + d:3 * w_c + 2 * d].astype(F32))
    m = (gf * jnp.dot(yf, wf_ref[...], preferred_element_type=F32)
         + gc * jnp.dot(yc, wc_ref[...], preferred_element_type=F32))
    o_ref[...] = x_ref[...] + jnp.dot(m.astype(BF16), wo_ref[...], preferred_element_type=F32)


def _mixer_out(seq_bounds, yf_a, yf_b, zr, x, conv_w, conv_b, w_fourier, w_conv_out, w_o):
    t, d = x.shape
    w_f = w_fourier.shape[0]
    w_c = w_conv_out.shape[0]
    tm = MIXER_TILE
    halo = 2 * V7X_SUBLANES
    n_a = yf_a.shape[0] // tm
    n_b = yf_b.shape[0] // tm
    nh = t // halo
    hb = tm // halo
    assert w_c % V7X_LANES == 0 and t % tm == 0 and all(b % tm == 0 for b in seq_bounds)
    edges = jnp.asarray([[0 if i * tm in seq_bounds else 1, 0 if (i + 1) * tm in seq_bounds else 1]
                         for i in range(t // tm)], dtype=jnp.int32)
    return pl.pallas_call(
        functools.partial(_mixer_out_kernel, n_a=n_a, w_c=w_c, d=d),
        grid_spec=pltpu.PrefetchScalarGridSpec(
            num_scalar_prefetch=1,
            grid=(t // tm,),
            in_specs=[
                pl.BlockSpec((tm, w_f), lambda i, e: (jnp.minimum(i, n_a - 1), 0)),
                pl.BlockSpec((tm, w_f), lambda i, e: (jnp.clip(i - n_a, 0, n_b - 1), 0)),
                pl.BlockSpec((tm, zr.shape[1]), lambda i, e: (i, 0)),
                pl.BlockSpec((halo, w_c), lambda i, e: (jnp.maximum(i * hb - 1, 0), 1)),
                pl.BlockSpec((halo, w_c), lambda i, e: (jnp.maximum(i * hb - 1, 0), 2)),
                pl.BlockSpec((halo, w_c), lambda i, e: (jnp.minimum((i + 1) * hb, nh - 1), 1)),
                pl.BlockSpec((halo, w_c), lambda i, e: (jnp.minimum((i + 1) * hb, nh - 1), 2)),
                pl.BlockSpec((tm, d), lambda i, e: (i, 0)),
                _const_spec((conv_w.shape[0], w_c)),
                _const_spec((1, w_c)),
                _const_spec((w_f, d)),
                _const_spec((w_c, d)),
                _const_spec((d, d)),
            ],
            out_specs=pl.BlockSpec((tm, d), lambda i, e: (i, 0)),
        ),
        out_shape=jax.ShapeDtypeStruct((t, d), F32),
        compiler_params=_params("parallel"),
        name="mixer_out",
    )(edges, yf_a, yf_b, zr, zr, zr, zr, zr, x, conv_w, conv_b.reshape(1, w_c),
      w_fourier, w_conv_out, w_o)


def _dense_ffn_kernel(x_ref, g_ref, wg_ref, wu_ref, wd_ref, fg_ref, o_ref, h_ref, *, final_norm):
    j = pl.program_id(1)

    @pl.when(j == 0)
    def _():
        h_ref[...] = _rmsnorm(x_ref[...], g_ref[...]).astype(BF16)
        o_ref[...] = x_ref[...]

    h = h_ref[...]
    a = jnp.dot(h, wg_ref[...], preferred_element_type=F32)
    b = jnp.dot(h, wu_ref[...], preferred_element_type=F32)
    act = (a * jax.nn.sigmoid(a) * b).astype(BF16)
    o_ref[...] += jnp.dot(act, wd_ref[...], preferred_element_type=F32)

    if final_norm:
        @pl.when(j == pl.num_programs(1) - 1)
        def _():
            o_ref[...] = _rmsnorm(o_ref[...], fg_ref[...])


def _dense_ffn(x, g, wg, wu, wd, final_g):
    t, d = x.shape
    f = wg.shape[1]
    tm, tf = TOKEN_TILE, FFN_COLS
    assert t % tm == 0 and f % tf == 0
    final_norm = final_g is not None
    fg = (final_g if final_norm else jnp.ones((d,), F32)).reshape(1, d)
    return pl.pallas_call(
        functools.partial(_dense_ffn_kernel, final_norm=final_norm),
        grid=(t // tm, f // tf),
        in_specs=[
            pl.BlockSpec((tm, d), lambda i, j: (i, 0)),
            pl.BlockSpec((1, d), lambda i, j: (0, 0)),
            pl.BlockSpec((d, tf), lambda i, j: (0, j)),
            pl.BlockSpec((d, tf), lambda i, j: (0, j)),
            pl.BlockSpec((tf, d), lambda i, j: (j, 0)),
            pl.BlockSpec((1, d), lambda i, j: (0, 0)),
        ],
        out_specs=pl.BlockSpec((tm, d), lambda i, j: (i, 0)),
        out_shape=jax.ShapeDtypeStruct((t, d), F32),
        scratch_shapes=[pltpu.VMEM((tm, d), BF16)],
        compiler_params=_params("parallel", "arbitrary"),
        name="dense_ffn",
    )(x, g.reshape(1, d), wg, wu, wd, fg)


def _router_kernel(x_ref, g_ref, rwh_ref, rwl_ref, tri_ref, ri_ref, rp_ref, cnt_ref, carry_ref, *, n_exp):
    @pl.when(pl.program_id(0) == 0)
    def _():
        carry_ref[...] = jnp.zeros_like(carry_ref)

    tm = x_ref.shape[0]
    lanes = rwh_ref.shape[1]
    h = _rmsnorm(x_ref[...], g_ref[...])
    h_hi = h.astype(BF16)
    h_lo = (h - h_hi.astype(F32)).astype(BF16)
    logits = (jnp.dot(h_hi, rwh_ref[...], preferred_element_type=F32)
              + jnp.dot(h_lo, rwh_ref[...], preferred_element_type=F32)
              + jnp.dot(h_hi, rwl_ref[...], preferred_element_type=F32))
    lane = lax.broadcasted_iota(jnp.int32, (tm, lanes), 1).astype(F32)
    neg = jnp.float32(-jnp.inf)
    lg = jnp.where(lane < n_exp, logits, neg)
    m1 = jnp.max(lg, axis=1, keepdims=True)
    i1 = jnp.min(jnp.where(lg == m1, lane, float(lanes)), axis=1, keepdims=True)
    lg2 = jnp.where(lane == i1, neg, lg)
    m2 = jnp.max(lg2, axis=1, keepdims=True)
    i2 = jnp.min(jnp.where(lg2 == m2, lane, float(lanes)), axis=1, keepdims=True)
    sel = jnp.where((lane == i1) | (lane == i2), 1.0, 0.0)
    before = jnp.dot(tri_ref[...], sel.astype(BF16), preferred_element_type=F32)
    rank = carry_ref[...] + before
    r1 = jnp.sum(jnp.where(lane == i1, rank, 0.0), axis=1, keepdims=True)
    r2 = jnp.sum(jnp.where(lane == i2, rank, 0.0), axis=1, keepdims=True)
    carry_ref[...] += jnp.sum(sel, axis=0, keepdims=True)
    cnt_ref[...] = carry_ref[...]
    e21 = jnp.exp(m2 - m1)
    p1 = 1.0 / (1.0 + e21)
    p2 = e21 / (1.0 + e21)
    rp_ref[...] = jnp.where(lane == 0, p1, jnp.where(lane == 1, p2, 0.0))
    q = jnp.where(lane == 0, i1, jnp.where(lane == 1, i2, jnp.where(lane == 2, r1, jnp.where(lane == 3, r2, 0.0))))
    ri_ref[...] = q.T[0:ri_ref.shape[0], :].astype(jnp.int32)


def _router(x, g, router_w):
    t, d = x.shape
    n_exp = router_w.shape[1]
    tm = TOKEN_TILE
    lanes = V7X_LANES
    rw = jnp.zeros((d, lanes), F32).at[:, :n_exp].set(router_w)
    rw_hi = rw.astype(BF16)
    rw_lo = (rw - rw_hi.astype(F32)).astype(BF16)
    tri =(lax.broadcasted_iota(jnp.int32, (tm, tm), 1)
           < lax.broadcasted_iota(jnp.int32, (tm, tm), 0)).astype(BF16)
    return pl.pallas_call(
        functools.partial(_router_kernel, n_exp=n_exp),
        grid=(t // tm,),
        in_specs=[
            pl.BlockSpec((tm, d), lambda i: (i, 0)),
            _const_spec((1, d)),
            _const_spec((d, lanes)),
            _const_spec((d, lanes)),
            _const_spec((tm, tm)),
        ],
        out_specs=[
            pl.BlockSpec((V7X_SUBLANES, tm), lambda i: (0, i)),
            pl.BlockSpec((tm, lanes), lambda i: (i, 0)),
            pl.BlockSpec((1, lanes), lambda i: (0, 0)),
        ],
        out_shape=[
            jax.ShapeDtypeStruct((V7X_SUBLANES, t), jnp.int32),
            jax.ShapeDtypeStruct((t, lanes), F32),
            jax.ShapeDtypeStruct((1, lanes), F32),
        ],
        scratch_shapes=[pltpu.VMEM((1, lanes), F32)],
        compiler_params=_params("arbitrary"),
        name="router",
    )(x, g.reshape(1, d), rw_hi, rw_lo, tri)


def _row_copy(src_ref, src_row, dst_ref, dst_row, sem):
    return pltpu.make_async_copy(src_ref.at[pl.ds(src_row, 1), :], dst_ref.at[pl.ds(dst_row, 1), :], sem)


def _dispatch_kernel(off_ref, tend_ref, nv_ref, x_ref, g_ref, ri_ref, hs_ref, h_ref, sem, zsem, *, n_tiles):
    tm = x_ref.shape[0]

    @pl.when(pl.program_id(0) == 0)
    def _():
        h_ref[...] = jnp.zeros_like(h_ref)

        def zero_tile(tile):
            cp = pltpu.make_async_copy(h_ref, hs_ref.at[pl.ds(tile * tm, tm), :], zsem)
            cp.start()
            cp.wait()

        for e in range(tend_ref.shape[0]):
            first = tend_ref[e - 1] if e else 0

            @pl.when(tend_ref[e] > first)
            def _():
                zero_tile(tend_ref[e] - 1)

        def tail(tile, c):
            zero_tile(tile)
            return c

        lax.fori_loop(nv_ref[0], n_tiles, tail, 0)

    h_ref[...] = _rmsnorm(x_ref[...], g_ref[...])

    def issue(s, c):
        for k in range(TOP_K):
            dst = off_ref[ri_ref[k, s]] + ri_ref[TOP_K + k, s]
            _row_copy(h_ref, s, hs_ref, dst, sem).start()
        return c

    lax.fori_loop(0, tm, issue, 0, unroll=DMA_ISSUE_UNROLL)
    for _ in range(TOP_K):
        pltpu.make_async_copy(h_ref, hs_ref.at[pl.ds(0, tm), :], sem).wait()


def _dispatch(x, g, route_i, offsets, tile_end, n_valid, n_tiles):
    t, d = x.shape
    tm = TOKEN_TILE
    assert tm == EXPERT_TILE
    return pl.pallas_call(
        functools.partial(_dispatch_kernel, n_tiles=n_tiles),
        grid_spec=pltpu.PrefetchScalarGridSpec(
            num_scalar_prefetch=3,
            grid=(t // tm,),
            in_specs=[
                pl.BlockSpec((tm, d), lambda i, *_: (i, 0)),
                _const_spec((1, d)),
                pl.BlockSpec((route_i.shape[0], tm), lambda i, *_: (0, i), memory_space=pltpu.SMEM),
            ],
            out_specs=pl.BlockSpec(memory_space=pl.ANY),
            scratch_shapes=[pltpu.VMEM((tm, d), F32), pltpu.SemaphoreType.DMA(()),
                            pltpu.SemaphoreType.DMA(())],
        ),
        out_shape=jax.ShapeDtypeStruct((n_tiles * EXPERT_TILE, d), F32),
        compiler_params=_params("arbitrary"),
        name="dispatch",
    )(offsets, tile_end, n_valid, x, g.reshape(1, d), route_i)


def _expert_ffn_kernel(te_ref, nv_ref, hs_ref, wg_ref, wu_ref, wd_ref, o_ref, xb_ref):
    del te_ref
    j = pl.program_id(1)
    used = pl.program_id(0) < nv_ref[0]

    @pl.when(jnp.logical_and(jnp.logical_not(used), j == 0))
    def _():
        o_ref[...] = jnp.zeros_like(o_ref)

    @pl.when(used)
    def _():
        @pl.when(j == 0)
        def _():
            xb_ref[...] = hs_ref[...].astype(BF16)

        xb = xb_ref[...]
        a = jnp.dot(xb, wg_ref[...], preferred_element_type=F32)
        b = jnp.dot(xb, wu_ref[...], preferred_element_type=F32)
        act = (a * jax.nn.sigmoid(a) * b).astype(BF16)
        y = jnp.dot(act, wd_ref[...], preferred_element_type=F32)

        @pl.when(j == 0)
        def _():
            o_ref[...] = y

        @pl.when(j > 0)
        def _():
            o_ref[...] += y


def _expert_ffn(hs, tile_expert, n_valid, wg, wu, wd):
    n_rows, d = hs.shape
    f = wg.shape[2]
    tm, tf = EXPERT_TILE, EXPERT_COLS
    assert n_rows % tm == 0 and f % tf == 0
    nj = f // tf

    def col(g, j, nv):
        return jnp.where(g < nv[0], j, nj - 1)

    return pl.pallas_call(
        _expert_ffn_kernel,
        grid_spec=pltpu.PrefetchScalarGridSpec(
            num_scalar_prefetch=2,
            grid=(n_rows // tm, nj),
            in_specs=[
                pl.BlockSpec((tm, d), lambda g, j, te, nv: (g, 0)),
                pl.BlockSpec((None, d, tf), lambda g, j, te, nv: (te[g], 0, col(g, j, nv))),
                pl.BlockSpec((None, d, tf), lambda g, j, te, nv: (te[g], 0, col(g, j, nv))),
                pl.BlockSpec((None, tf, d), lambda g, j, te, nv: (te[g], col(g, j, nv), 0)),
            ],
            out_specs=pl.BlockSpec((tm, d), lambda g, j, te, nv: (g, 0)),
            scratch_shapes=[pltpu.VMEM((tm, d), BF16)],
        ),
        out_shape=jax.ShapeDtypeStruct((n_rows, d), F32),
        compiler_params=_params("arbitrary", "arbitrary"),
        name="expert_ffn",
    )(tile_expert, n_valid, hs, wg, wu, wd)


def _combine_kernel(off_ref, x_ref, ri_ref, rp_ref, fg_ref, ys_ref, o_ref, y_ref, sem, *, final_norm):
    tm = x_ref.shape[0]

    def issue(s, c):
        for k in range(TOP_K):
            src = off_ref[ri_ref[k, s]] + ri_ref[TOP_K + k, s]
            _row_copy(ys_ref, src, y_ref.at[k], s, sem).start()
        return c

    lax.fori_loop(0, tm, issue, 0, unroll=DMA_ISSUE_UNROLL)
    for k in range(TOP_K):
        pltpu.make_async_copy(ys_ref.at[pl.ds(0, tm), :], y_ref.at[k], sem).wait()

    y = x_ref[...] + rp_ref[:, 0:1] * y_ref[0] + rp_ref[:, 1:2] * y_ref[1]
    if final_norm:
        y = _rmsnorm(y, fg_ref[...])
    o_ref[...] = y


def _combine(x, route_i, route_p, offsets, ys, final_g, row0, rows):
    t, d = x.shape
    tm = TOKEN_TILE
    assert row0 % tm == 0 and rows % tm == 0
    b0 = row0 // tm
    final_norm = final_g is not None
    fg = (final_g if final_norm else jnp.ones((d,), F32)).reshape(1, d)
    return pl.pallas_call(
        functools.partial(_combine_kernel, final_norm=final_norm),
        grid_spec=pltpu.PrefetchScalarGridSpec(
            num_scalar_prefetch=1,
            grid=(rows // tm,),
            in_specs=[
                pl.BlockSpec((tm, d), lambda i, o: (b0 + i, 0)),
                pl.BlockSpec((route_i.shape[0], tm), lambda i, o: (0, b0 + i), memory_space=pltpu.SMEM),
                pl.BlockSpec((tm, route_p.shape[1]), lambda i, o: (b0 + i, 0)),
                _const_spec((1, d)),
                pl.BlockSpec(memory_space=pl.ANY),
            ],
            out_specs=pl.BlockSpec((tm, d), lambda i, o: (i, 0)),
            scratch_shapes=[pltpu.VMEM((TOP_K, tm, d), F32), pltpu.SemaphoreType.DMA(())],
        ),
        out_shape=jax.ShapeDtypeStruct((rows, d), F32),
        compiler_params=_params("arbitrary"),
        name="combine",
    )(offsets, x, route_i, route_p, fg, ys)


def _moe_plan(counts, n_exp, n_tiles):
    tm = EXPERT_TILE
    cnt = counts[0, :n_exp].astype(jnp.int32)
    tiles = (cnt + tm - 1) // tm
    tile_end = jnp.cumsum(tiles)
    offsets = (tile_end - tiles) * tm
    n_valid = tile_end[-1]
    g = jnp.arange(n_tiles, dtype=jnp.int32)
    gc = jnp.minimum(g, n_valid - 1)
    tile_expert = jnp.sum((gc[:, None] >= tile_end[None, :]).astype(jnp.int32), axis=1)
    return offsets, tile_end, tile_expert, n_valid.reshape(1)


def _moe_layer(x, g, router_w, wg, wu, wd, final_g, splits):
    t, d = x.shape
    n_exp = router_w.shape[1]
    n_tiles = (t * TOP_K) // EXPERT_TILE + n_exp
    route_i, route_p, counts = _router(x, g, router_w)
    offsets, tile_end, tile_expert, n_valid = _moe_plan(counts, n_exp, n_tiles)
    hs = _dispatch(x, g, route_i, offsets, tile_end, n_valid, n_tiles)
    ys = _expert_ffn(hs, tile_expert, n_valid, wg, wu, wd)
    return [_combine(x, route_i, route_p, offsets, ys, final_g, r0, n) for r0, n in splits]


def kernel(x_prompt, x_sample, norm1_g, w_in, conv_w, conv_b, w_fourier, w_conv_out, w_o, norm2_g,
           dense_w_gate, dense_w_up, dense_w_down, router_w, moe_w_gate, moe_w_up, moe_w_down, final_g):
    bp, sp, d = x_prompt.shape
    bs, ss, _ = x_sample.shape
    tp, ts = bp * sp, bs * ss
    t = tp + ts
    depth = norm1_g.shape[0]
    w_f = w_fourier.shape[1]
    gd = w_f // F_GROUPS
    tm = TOKEN_TILE
    assert sp % tm == 0 and ss % tm == 0

    x = jnp.concatenate([x_prompt.reshape(tp, d), x_sample.reshape(ts, d)], axis=0)

    seq_bounds = frozenset([b * sp for b in range(bp + 1)] + [tp + b * ss for b in range(bs + 1)])

    cc, sc = _dft_tables(gd)
    csc = jnp.concatenate([cc, sc], axis=1).astype(BF16)

    for l in range(depth):
        zf, zr = _norm_inproj(x, norm1_g[l], w_in[l].astype(BF16), w_f)
        yf_p = _dft_direct(zf, bp, sp, gd, csc)
        yf_s = _dft_two_stage(zf, tp, bs, ss, gd, csc)
        x = _mixer_out(seq_bounds, yf_p, yf_s, zr, x, conv_w[l], conv_b[l], w_fourier[l].astype(BF16),
                       w_conv_out[l].astype(BF16), w_o[l].astype(BF16))
        last = l == depth - 1
        fg = final_g if last else None
        j = l // 2
        if l % 2 == 0:
            x = _dense_ffn(x, norm2_g[l], dense_w_gate[j].astype(BF16), dense_w_up[j].astype(BF16),
                           dense_w_down[j].astype(BF16), fg)
            if last:
                outs = [x[:tp], x[tp:]]
        else:
            splits = [(0, tp), (tp, ts)] if last else [(0, t)]
            outs = _moe_layer(x, norm2_g[l], router_w[j], moe_w_gate[j].astype(BF16),
                              moe_w_up[j].astype(BF16), moe_w_down[j].astype(BF16), fg, splits)
            if not last:
                x = outs[0]
    return outs[0].reshape(bp, sp, d), outs[1].reshape(bs, ss, d)
```

```python
import functools
import math

import jax
import jax.numpy as jnp
from jax import lax
from jax.experimental import pallas as pl
from jax.experimental.pallas import tpu as pltpu

F_GROUPS = 8
TOP_K = 2
EPS = 1e-6
BF16 = jnp.bfloat16
F32 = jnp.float32

V7X_LANES = 128
V7X_SUBLANES = 8
V7X_VMEM_BYTES = 64 * 1024 * 1024
VMEM_LIMIT = V7X_VMEM_BYTES - 8 * 1024 * 1024

TOKEN_TILE = 512
MIXER_TILE = 256
INPROJ_TILE = 1024
INPROJ_COLS = 1024
FFN_COLS = 512
EXPERT_TILE = 512
EXPERT_COLS = 1024
DMA_ISSUE_UNROLL = 8
DFT_RADIX = 128


def _params(*sem):
    return pltpu.CompilerParams(dimension_semantics=sem, vmem_limit_bytes=VMEM_LIMIT)


def _const_spec(shape):
    return pl.BlockSpec(shape, lambda *_: (0,) * len(shape), pipeline_mode=pl.Buffered(1))


def _rmsnorm(x, g):
    return x * lax.rsqrt(jnp.mean(x * x, axis=-1, keepdims=True) + EPS) * g


def _norm_inproj_kernel(x_ref, g_ref, w_ref, zf_ref, zr_ref, h_ref):
    j = pl.program_id(1)

    @pl.when(j == 0)
    def _():
        h_ref[...] = _rmsnorm(x_ref[...], g_ref[...]).astype(BF16)
        zf_ref[...] = jnp.dot(h_ref[...], w_ref[...], preferred_element_type=F32)

    @pl.when(j > 0)
    def _():
        zr_ref[...] = jnp.dot(h_ref[...], w_ref[...], preferred_element_type=F32).astype(BF16)


def _norm_inproj(x, g, w, w_f):
    t, d = x.shape
    n_in = w.shape[1]
    tm, tn = INPROJ_TILE, INPROJ_COLS
    assert w_f == tn and t % tm == 0 and (n_in - w_f) % tn == 0
    return pl.pallas_call(
        _norm_inproj_kernel,
        grid=(t // tm, n_in // tn),
        in_specs=[
            pl.BlockSpec((tm, d), lambda i, j: (i, 0)),
            pl.BlockSpec((1, d), lambda i, j: (0, 0)),
            pl.BlockSpec((d, tn), lambda i, j: (0, j)),
        ],
        out_specs=[
            pl.BlockSpec((tm, tn), lambda i, j: (i, 0)),
            pl.BlockSpec((tm, tn), lambda i, j: (i, jnp.maximum(j - 1, 0))),
        ],
        out_shape=[
            jax.ShapeDtypeStruct((t, w_f), F32),
            jax.ShapeDtypeStruct((t, n_in - w_f), BF16),
        ],
        scratch_shapes=[pltpu.VMEM((tm, d), BF16)],
        compiler_params=_params("parallel", "arbitrary"),
        name="norm_inproj",
    )(x, g.reshape(1, d), w)


def _dft_tables(n):
    j = lax.broadcasted_iota(jnp.int32, (n, n), 0)
    k = lax.broadcasted_iota(jnp.int32, (n, n), 1)
    ang = ((j * k) % n).astype(F32) * (2.0 * math.pi / n)
    return jnp.cos(ang), jnp.sin(ang)


def _dft_direct_kernel(x_ref, csc_ref, css_ref, o_ref, ab_ref, *, gd):
    s = x_ref.shape[0]

    @pl.when(pl.program_id(1) == 0)
    def _():
        for g in range(x_ref.shape[1] // gd):
            cols = slice(g * gd, (g + 1) * gd)
            ab = jnp.dot(x_ref[:, cols].astype(BF16), csc_ref[...], preferred_element_type=F32)
            ab_ref[0:s, cols] = ab[:, :gd].astype(BF16)
            ab_ref[s:2 * s, cols] = ab[:, gd:].astype(BF16)

    o_ref[...] = jnp.dot(css_ref[...], ab_ref[...], preferred_element_type=F32)


def _dft_direct(zf, nseq, s, gd, csc):
    w_f = zf.shape[1]
    tm = TOKEN_TILE
    cs, sn = _dft_tables(s)
    scale = 1.0 / math.sqrt(s * gd)
    css = (jnp.concatenate([cs, -sn], axis=1) * scale).astype(BF16)
    return pl.pallas_call(
        functools.partial(_dft_direct_kernel, gd=gd),
        grid=(nseq, s // tm),
        in_specs=[
            pl.BlockSpec((s, w_f), lambda b, i: (b, 0)),
            _const_spec((gd, 2 * gd)),
            pl.BlockSpec((tm, 2 * s), lambda b, i: (i, 0)),
        ],
        out_specs=pl.BlockSpec((tm, w_f), lambda b, i: (b * (s // tm) + i, 0)),
        out_shape=jax.ShapeDtypeStruct((nseq * s, w_f), F32),
        scratch_shapes=[pltpu.VMEM((2 * s, w_f), BF16)],
        compiler_params=_params("parallel", "arbitrary"),
        name="dft_direct",
    )(zf, csc, css)


def _dft_stage1_kernel(x_ref, csc_ref, m1_ref, twc_ref, tws_ref, tr_ref, ti_ref,
                       x2, tr2, ti2, xs_ref, ab_ref, *, n1, gd):
    nf = x_ref.shape[1]
    for g in range(x_ref.shape[2] // gd):
        cols = slice(g * gd, (g + 1) * gd)
        x2[...] = x_ref[:, :, cols].reshape(n1 * nf, gd)
        for f in range(nf):
            xs_ref[f * n1:(f + 1) * n1, :] = x2[pl.ds(f, n1, stride=nf), :].astype(BF16)
        ab = jnp.dot(xs_ref[...], csc_ref[...], preferred_element_type=F32)
        for f in range(nf):
            ab_ref[0:n1, f * gd:(f + 1) * gd] = ab[f * n1:(f + 1) * n1, :gd].astype(BF16)
            ab_ref[n1:2 * n1, f * gd:(f + 1) * gd] = ab[f * n1:(f + 1) * n1, gd:].astype(BF16)
        gq = jnp.dot(m1_ref[...], ab_ref[...], preferred_element_type=F32)
        for f in range(nf):
            gr = gq[0:n1, f * gd:(f + 1) * gd]
            gi = gq[n1:2 * n1, f * gd:(f + 1) * gd]
            tc = twc_ref[f]
            ts = tws_ref[f]
            tr2[pl.ds(f, n1, stride=nf), :] = gr * tc + gi * ts
            ti2[pl.ds(f, n1, stride=nf), :] = gi * tc - gr * ts
        tr_ref[:, :, cols] = tr2[...].reshape(n1, nf, gd)
        ti_ref[:, :, cols] = ti2[...].reshape(n1, nf, gd)


def _dft_stage2_kernel(tr_ref, ti_ref, c2s2_ref, o_ref, o2, tcat_ref, *, n2, gd):
    nk = tr_ref.shape[0]
    for g in range(tr_ref.shape[2] // gd):
        cols = slice(g * gd, (g + 1) * gd)
        for q in range(nk):
            tcat_ref[0:n2, q * gd:(q + 1) * gd] = tr_ref[q, :, cols].astype(BF16)
            tcat_ref[n2:2 * n2, q * gd:(q + 1) * gd] = ti_ref[q, :, cols].astype(BF16)
        y = jnp.dot(c2s2_ref[...], tcat_ref[...], preferred_element_type=F32)
        for q in range(nk):
            o2[pl.ds(q, n2, stride=nk), :] = y[:, q * gd:(q + 1) * gd]
        o_ref[:, :, cols] = o2[...].reshape(n2, nk, gd)


def _dft_two_stage(zf, row0, nseq, s, gd, csc):
    w_f = zf.shape[1]
    n1 = DFT_RADIX
    n2 = s // n1
    nf = V7X_SUBLANES
    assert s % n1 == 0 and n2 % nf == 0 and n1 % nf == 0 and gd == V7X_LANES
    if row0 % s == 0:
        blk0 = row0 // s
    else:
        zf, blk0 = zf[row0:], 0
    scale = 1.0 / math.sqrt(s * gd)

    c1, s1 = _dft_tables(n1)
    m1 = jnp.concatenate([jnp.concatenate([c1, -s1], axis=1),
                          jnp.concatenate([-s1, -c1], axis=1)], axis=0).astype(BF16)
    a = lax.broadcasted_iota(jnp.int32, (n2, n1), 0)
    b = lax.broadcasted_iota(jnp.int32, (n2, n1), 1)
    ang = ((a * b) % s).astype(F32) * (2.0 * math.pi / s)
    twc = jnp.broadcast_to((jnp.cos(ang) * scale)[:, :, None], (n2, n1, gd))
    tws = jnp.broadcast_to((jnp.sin(ang) * scale)[:, :, None], (n2, n1, gd))
    c2, s2 = _dft_tables(n2)
    c2s2 = jnp.concatenate([c2, s2], axis=1).astype(BF16)

    x3 = zf.reshape(zf.shape[0] // n2, n2, w_f)
    t_shape = jax.ShapeDtypeStruct((nseq * n1, n2, w_f), F32)
    tr, ti = pl.pallas_call(
        functools.partial(_dft_stage1_kernel, n1=n1, gd=gd),
        grid=(nseq, n2 // nf),
        in_specs=[
            pl.BlockSpec((n1, nf, w_f), lambda b, f: (blk0 + b, f, 0)),
            _const_spec((gd, 2 * gd)),
            _const_spec((2 * n1, 2 * n1)),
            pl.BlockSpec((nf, n1, gd), lambda b, f: (f, 0, 0)),
            pl.BlockSpec((nf, n1, gd), lambda b, f: (f, 0, 0)),
        ],
        out_specs=[
            pl.BlockSpec((n1, nf, w_f), lambda b, f: (b, f, 0)),
            pl.BlockSpec((n1, nf, w_f), lambda b, f: (b, f, 0)),
        ],
        out_shape=[t_shape, t_shape],
        scratch_shapes=[pltpu.VMEM((n1 * nf, gd), F32), pltpu.VMEM((n1 * nf, gd), F32),
                        pltpu.VMEM((n1 * nf, gd), F32),
                        pltpu.VMEM((nf * n1, gd), BF16), pltpu.VMEM((2 * n1, nf * gd), BF16)],
        compiler_params=_params("parallel", "parallel"),
        name="dft_stage1",
    )(x3, csc, m1, twc, tws)

    y = pl.pallas_call(
        functools.partial(_dft_stage2_kernel, n2=n2, gd=gd),
        grid=(nseq, n1 // nf),
        in_specs=[
            pl.BlockSpec((nf, n2, w_f), lambda b, q: (b * (n1 // nf) + q, 0, 0)),
            pl.BlockSpec((nf, n2, w_f), lambda b, q: (b * (n1 // nf) + q, 0, 0)),
            _const_spec((n2, 2 * n2)),
        ],
        out_specs=pl.BlockSpec((n2, nf, w_f), lambda b, q: (b, q, 0)),
        out_shape=jax.ShapeDtypeStruct((nseq * n2, n1, w_f), F32),
        scratch_shapes=[pltpu.VMEM((n2 * nf, gd), F32), pltpu.VMEM((2 * n2, nf * gd), BF16)],
        compiler_params=_params("parallel", "parallel"),
        name="dft_stage2",
    )(tr, ti, c2s2)
    return y.reshape(nseq * s, w_f)


def _mixer_out_kernel(edge_ref, yfa_ref, yfb_ref, zr_ref, pc_ref, pv_ref, nc_ref, nv_ref, x_ref,
                      cw_ref, cb_ref, wf_ref, wc_ref, wo_ref, o_ref, *, n_a, w_c, d):
    i = pl.program_id(0)
    tm = x_ref.shape[0]
    halo = pc_ref.shape[0]
    zb = zr_ref[:, 0:w_c].astype(F32)
    u = zr_ref[:, w_c:2 * w_c].astype(F32) * zr_ref[:, 2 * w_c:3 * w_c].astype(F32)
    u_prev = pc_ref[halo - 1:halo, :].astype(F32) * pv_ref[halo - 1:halo, :].astype(F32)
    u_next = nc_ref[0:1, :].astype(F32) * nv_ref[0:1, :].astype(F32)
    u_prev = jnp.where(edge_ref[i, 0] > 0, u_prev, 0.0)
    u_next = jnp.where(edge_ref[i, 1] > 0, u_next, 0.0)
    row = lax.broadcasted_iota(jnp.int32, (tm, w_c), 0)
    u_m1 = jnp.where(row == 0, u_prev, pltpu.roll(u, 1, 0))
    u_p1 = jnp.where(row == tm - 1, u_next, pltpu.roll(u, tm - 1, 0))
    cv = u_m1 * cw_ref[0:1, :] + u * cw_ref[1:2, :] + u_p1 * cw_ref[2:3, :] + cb_ref[...]
    yc = (zb * cv).astype(BF16)
    yf = jnp.where(i < n_a, yfa_ref[...], yfb_ref[...]).astype(BF16)
    gf = jax.nn.sigmoid(zr_ref[:, 3 * w_c:3 * w_c + d].astype(F32))
    gc = jax.nn.sigmoid(zr_ref[:, 3 * w_c + d:3 * w_c + 2 * d].astype(F32))
    m = (gf * jnp.dot(yf, wf_ref[...], preferred_element_type=F32)
         + gc * jnp.dot(yc, wc_ref[...], preferred_element_type=F32))
    o_ref[...] = x_ref[...] + jnp.dot(m.astype(BF16), wo_ref[...], preferred_element_type=F32)


def _mixer_out(seq_bounds, yf_a, yf_b, zr, x, conv_w, conv_b, w_fourier, w_conv_out, w_o):
    t, d = x.shape
    w_f = w_fourier.shape[0]
    w_c = w_conv_out.shape[0]
    tm = MIXER_TILE
    halo = 2 * V7X_SUBLANES
    n_a = yf_a.shape[0] // tm
    n_b = yf_b.shape[0] // tm
    nh = t // halo
    hb = tm // halo
    assert w_c % V7X_LANES == 0 and t % tm == 0 and all(b % tm == 0 for b in seq_bounds)
    edges = jnp.asarray([[0 if i * tm in seq_bounds else 1, 0 if (i + 1) * tm in seq_bounds else 1]
                         for i in range(t // tm)], dtype=jnp.int32)
    return pl.pallas_call(
        functools.partial(_mixer_out_kernel, n_a=n_a, w_c=w_c, d=d),
        grid_spec=pltpu.PrefetchScalarGridSpec(
            num_scalar_prefetch=1,
            grid=(t // tm,),
            in_specs=[
                pl.BlockSpec((tm, w_f), lambda i, e: (jnp.minimum(i, n_a - 1), 0)),
                pl.BlockSpec((tm, w_f), lambda i, e: (jnp.clip(i - n_a, 0, n_b - 1), 0)),
                pl.BlockSpec((tm, zr.shape[1]), lambda i, e: (i, 0)),
                pl.BlockSpec((halo, w_c), lambda i, e: (jnp.maximum(i * hb - 1, 0), 1)),
                pl.BlockSpec((halo, w_c), lambda i, e: (jnp.maximum(i * hb - 1, 0), 2)),
                pl.BlockSpec((halo, w_c), lambda i, e: (jnp.minimum((i + 1) * hb, nh - 1), 1)),
                pl.BlockSpec((halo, w_c), lambda i, e: (jnp.minimum((i + 1) * hb, nh - 1), 2)),
                pl.BlockSpec((tm, d), lambda i, e: (i, 0)),
                _const_spec((conv_w.shape[0], w_c)),
                _const_spec((1, w_c)),
                _const_spec((w_f, d)),
                _const_spec((w_c, d)),
                _const_spec((d, d)),
            ],
            out_specs=pl.BlockSpec((tm, d), lambda i, e: (i, 0)),
        ),
        out_shape=jax.ShapeDtypeStruct((t, d), F32),
        compiler_params=_params("parallel"),
        name="mixer_out",
    )(edges, yf_a, yf_b, zr, zr, zr, zr, zr, x, conv_w, conv_b.reshape(1, w_c),
      w_fourier, w_conv_out, w_o)


def _dense_ffn_kernel(*refs, final_norm, n_cast):
    x_ref, g_ref, wg_ref, wu_ref, wd_ref, fg_ref = refs[:6]
    cast_in = refs[6:6 + n_cast]
    o_ref = refs[6 + n_cast]
    cast_out = refs[7 + n_cast:7 + 2 * n_cast]
    h_ref = refs[7 + 2 * n_cast]
    j = pl.program_id(1)

    @pl.when(j == 0)
    def _():
        h_ref[...] = _rmsnorm(x_ref[...], g_ref[...]).astype(BF16)
        o_ref[...] = x_ref[...]

    h = h_ref[...]
    a = jnp.dot(h, wg_ref[...], preferred_element_type=F32)
    b = jnp.dot(h, wu_ref[...], preferred_element_type=F32)
    act = (a * jax.nn.sigmoid(a) * b).astype(BF16)
    o_ref[...] += jnp.dot(act, wd_ref[...], preferred_element_type=F32)

    if final_norm:
        @pl.when(j == pl.num_programs(1) - 1)
        def _():
            o_ref[...] = _rmsnorm(o_ref[...], fg_ref[...])

    for src, dst in zip(cast_in, cast_out):
        dst[...] = src[...].astype(BF16)


def _cast_chunk_rows(rows, n_steps):
    step = 2 * V7X_SUBLANES
    chunk = step
    while rows % chunk or rows // chunk > n_steps:
        chunk += step
        assert chunk <= rows
    return chunk


def _dense_ffn(x, g, wg, wu, wd, final_g, cast_jobs=()):
    t, d = x.shape
    f = wg.shape[1]
    tm, tf = TOKEN_TILE, FFN_COLS
    assert t % tm == 0 and f % tf == 0
    nj = f // tf
    n_steps = (t // tm) * nj
    final_norm = final_g is not None
    fg = (final_g if final_norm else jnp.ones((d,), F32)).reshape(1, d)

    def cast_specs():
        specs = []
        for w in cast_jobs:
            rows, cols = w.shape
            chunk = _cast_chunk_rows(rows, n_steps)
            last = rows // chunk - 1
            specs.append(pl.BlockSpec(
                (chunk, cols), lambda i, j, last=last: (jnp.minimum(i * nj + j, last), 0)))
        return specs

    outs = pl.pallas_call(
        functools.partial(_dense_ffn_kernel, final_norm=final_norm, n_cast=len(cast_jobs)),
        grid=(t // tm, nj),
        in_specs=[
            pl.BlockSpec((tm, d), lambda i, j: (i, 0)),
            pl.BlockSpec((1, d), lambda i, j: (0, 0)),
            pl.BlockSpec((d, tf), lambda i, j: (0, j)),
            pl.BlockSpec((d, tf), lambda i, j: (0, j)),
            pl.BlockSpec((tf, d), lambda i, j: (j, 0)),
            pl.BlockSpec((1, d), lambda i, j: (0, 0)),
        ] + cast_specs(),
        out_specs=[pl.BlockSpec((tm, d), lambda i, j: (i, 0))] + cast_specs(),
        out_shape=[jax.ShapeDtypeStruct((t, d), F32)]
        + [jax.ShapeDtypeStruct(w.shape, BF16) for w in cast_jobs],
        scratch_shapes=[pltpu.VMEM((tm, d), BF16)],
        compiler_params=_params("arbitrary", "arbitrary"),
        name="dense_ffn",
    )(x, g.reshape(1, d), wg, wu, wd, fg, *cast_jobs)
    return outs[0], list(outs[1:])


def _router_kernel(x_ref, g_ref, rwh_ref, rwl_ref, tri_ref, ri_ref, rp_ref, cnt_ref, carry_ref, *, n_exp):
    @pl.when(pl.program_id(0) == 0)
    def _():
        carry_ref[...] = jnp.zeros_like(carry_ref)

    tm = x_ref.shape[0]
    lanes = rwh_ref.shape[1]
    h = _rmsnorm(x_ref[...], g_ref[...])
    h_hi = h.astype(BF16)
    h_lo = (h - h_hi.astype(F32)).astype(BF16)
    logits = (jnp.dot(h_hi, rwh_ref[...], preferred_element_type=F32)
              + jnp.dot(h_lo, rwh_ref[...], preferred_element_type=F32)
              + jnp.dot(h_hi, rwl_ref[...], preferred_element_type=F32))
    lane = lax.broadcasted_iota(jnp.int32, (tm, lanes), 1).astype(F32)
    neg = jnp.float32(-jnp.inf)
    lg = jnp.where(lane < n_exp, logits, neg)
    m1 = jnp.max(lg, axis=1, keepdims=True)
    i1 = jnp.min(jnp.where(lg == m1, lane, float(lanes)), axis=1, keepdims=True)
    lg2 = jnp.where(lane == i1, neg, lg)
    m2 = jnp.max(lg2, axis=1, keepdims=True)
    i2 = jnp.min(jnp.where(lg2 == m2, lane, float(lanes)), axis=1, keepdims=True)
    sel = jnp.where((lane == i1) | (lane == i2), 1.0, 0.0)
    before = jnp.dot(tri_ref[...], sel.astype(BF16), preferred_element_type=F32)
    rank = carry_ref[...] + before
    r1 = jnp.sum(jnp.where(lane == i1, rank, 0.0), axis=1, keepdims=True)
    r2 = jnp.sum(jnp.where(lane == i2, rank, 0.0), axis=1, keepdims=True)
    carry_ref[...] += jnp.sum(sel, axis=0, keepdims=True)
    cnt_ref[...] = carry_ref[...]
    e21 = jnp.exp(m2 - m1)
    p1 = 1.0 / (1.0 + e21)
    p2 = e21 / (1.0 + e21)
    rp_ref[...] = jnp.where(lane == 0, p1, jnp.where(lane == 1, p2, 0.0))
    q = jnp.where(lane == 0, i1, jnp.where(lane == 1, i2, jnp.where(lane == 2, r1, jnp.where(lane == 3, r2, 0.0))))
    ri_ref[...] = q.T[0:ri_ref.shape[0], :].astype(jnp.int32)


def _router(x, g, router_w):
    t, d = x.shape
    n_exp = router_w.shape[1]
    tm = TOKEN_TILE
    lanes = V7X_LANES
    rw = jnp.zeros((d, lanes), F32).at[:, :n_exp].set(router_w)
    rw_hi = rw.astype(BF16)
    rw_lo = (rw - rw_hi.astype(F32)).astype(BF16)
    tri =(lax.broadcasted_iota(jnp.int32, (tm, tm), 1)
           < lax.broadcasted_iota(jnp.int32, (tm, tm), 0)).astype(BF16)
    return pl.pallas_call(
        functools.partial(_router_kernel, n_exp=n_exp),
        grid=(t // tm,),
        in_specs=[
            pl.BlockSpec((tm, d), lambda i: (i, 0)),
            _const_spec((1, d)),
            _const_spec((d, lanes)),
            _const_spec((d, lanes)),
            _const_spec((tm, tm)),
        ],
        out_specs=[
            pl.BlockSpec((V7X_SUBLANES, tm), lambda i: (0, i)),
            pl.BlockSpec((tm, lanes), lambda i: (i, 0)),
            pl.BlockSpec((1, lanes), lambda i: (0, 0)),
        ],
        out_shape=[
            jax.ShapeDtypeStruct((V7X_SUBLANES, t), jnp.int32),
            jax.ShapeDtypeStruct((t, lanes), F32),
            jax.ShapeDtypeStruct((1, lanes), F32),
        ],
        scratch_shapes=[pltpu.VMEM((1, lanes), F32)],
        compiler_params=_params("arbitrary"),
        name="router",
    )(x, g.reshape(1, d), rw_hi, rw_lo, tri)


def _row_copy(src_ref, src_row, dst_ref, dst_row, sem):
    return pltpu.make_async_copy(src_ref.at[pl.ds(src_row, 1), :], dst_ref.at[pl.ds(dst_row, 1), :], sem)


def _dispatch_kernel(tend_ref, nv_ref, x_ref, g_ref, slot_ref, hs_ref, h_ref, sem, zsem, *, n_tiles):
    tm = x_ref.shape[0]

    @pl.when(pl.program_id(0) == 0)
    def _():
        h_ref[...] = jnp.zeros_like(h_ref)

        def zero_tile(tile):
            cp = pltpu.make_async_copy(h_ref, hs_ref.at[pl.ds(tile * tm, tm), :], zsem)
            cp.start()
            cp.wait()

        for e in range(tend_ref.shape[0]):
            first = tend_ref[e - 1] if e else 0

            @pl.when(tend_ref[e] > first)
            def _():
                zero_tile(tend_ref[e] - 1)

        def tail(tile, c):
            zero_tile(tile)
            return c

        lax.fori_loop(nv_ref[0], n_tiles, tail, 0)

    h_ref[...] = _rmsnorm(x_ref[...], g_ref[...])

    def issue(s, c):
        for k in range(TOP_K):
            _row_copy(h_ref, s, hs_ref, slot_ref[k * tm + s], sem).start()
        return c

    lax.fori_loop(0, tm, issue, 0, unroll=DMA_ISSUE_UNROLL)
    for _ in range(TOP_K):
        pltpu.make_async_copy(h_ref, hs_ref.at[pl.ds(0, tm), :], sem).wait()


def _dispatch(x, g, slots, tile_end, n_valid, n_tiles):
    t, d = x.shape
    tm = TOKEN_TILE
    assert tm == EXPERT_TILE
    return pl.pallas_call(
        functools.partial(_dispatch_kernel, n_tiles=n_tiles),
        grid_spec=pltpu.PrefetchScalarGridSpec(
            num_scalar_prefetch=2,
            grid=(t // tm,),
            in_specs=[
                pl.BlockSpec((tm, d), lambda i, *_: (i, 0)),
                _const_spec((1, d)),
                pl.BlockSpec((TOP_K * tm,), lambda i, *_: (i,), memory_space=pltpu.SMEM),
            ],
            out_specs=pl.BlockSpec(memory_space=pl.ANY),
            scratch_shapes=[pltpu.VMEM((tm, d), F32), pltpu.SemaphoreType.DMA(()),
                            pltpu.SemaphoreType.DMA(())],
        ),
        out_shape=jax.ShapeDtypeStruct((n_tiles * EXPERT_TILE, d), F32),
        compiler_params=_params("arbitrary"),
        name="dispatch",
    )(tile_end, n_valid, x, g.reshape(1, d), slots)


def _expert_ffn_kernel(te_ref, nv_ref, hs_ref, wg_ref, wu_ref, wd_ref, o_ref, xb_ref):
    del te_ref
    j = pl.program_id(1)
    used = pl.program_id(0) < nv_ref[0]

    @pl.when(jnp.logical_and(jnp.logical_not(used), j == 0))
    def _():
        o_ref[...] = jnp.zeros_like(o_ref)

    @pl.when(used)
    def _():
        @pl.when(j == 0)
        def _():
            xb_ref[...] = hs_ref[...].astype(BF16)

        xb = xb_ref[...]
        a = jnp.dot(xb, wg_ref[...], preferred_element_type=F32)
        b = jnp.dot(xb, wu_ref[...], preferred_element_type=F32)
        act = (a * jax.nn.sigmoid(a) * b).astype(BF16)
        y = jnp.dot(act, wd_ref[...], preferred_element_type=F32)

        @pl.when(j == 0)
        def _():
            o_ref[...] = y

        @pl.when(j > 0)
        def _():
            o_ref[...] += y


def _expert_ffn(hs, tile_expert, n_valid, wg, wu, wd):
    n_rows, d = hs.shape
    f = wg.shape[2]
    tm, tf = EXPERT_TILE, EXPERT_COLS
    assert n_rows % tm == 0 and f % tf == 0
    nj = f // tf

    def col(g, j, nv):
        return jnp.where(g < nv[0], j, nj - 1)

    return pl.pallas_call(
        _expert_ffn_kernel,
        grid_spec=pltpu.PrefetchScalarGridSpec(
            num_scalar_prefetch=2,
            grid=(n_rows // tm, nj),
            in_specs=[
                pl.BlockSpec((tm, d), lambda g, j, te, nv: (g, 0)),
                pl.BlockSpec((None, d, tf), lambda g, j, te, nv: (te[g], 0, col(g, j, nv))),
                pl.BlockSpec((None, d, tf), lambda g, j, te, nv: (te[g], 0, col(g, j, nv))),
                pl.BlockSpec((None, tf, d), lambda g, j, te, nv: (te[g], col(g, j, nv), 0)),
            ],
            out_specs=pl.BlockSpec((tm, d), lambda g, j, te, nv: (g, 0)),
            scratch_shapes=[pltpu.VMEM((tm, d), BF16)],
        ),
        out_shape=jax.ShapeDtypeStruct((n_rows, d), F32),
        compiler_params=_params("arbitrary", "arbitrary"),
        name="expert_ffn",
    )(tile_expert, n_valid, hs, wg, wu, wd)


def _combine_kernel(x_ref, slot_ref, rp_ref, fg_ref, ys_ref, o_ref, y_ref, sem, *, final_norm):
    tm = x_ref.shape[0]

    def issue(s, c):
        for k in range(TOP_K):
            _row_copy(ys_ref, slot_ref[k * tm + s], y_ref.at[k], s, sem).start()
        return c

    lax.fori_loop(0, tm, issue, 0, unroll=DMA_ISSUE_UNROLL)
    for k in range(TOP_K):
        pltpu.make_async_copy(ys_ref.at[pl.ds(0, tm), :], y_ref.at[k], sem).wait()

    y = x_ref[...] + rp_ref[:, 0:1] * y_ref[0] + rp_ref[:, 1:2] * y_ref[1]
    if final_norm:
        y = _rmsnorm(y, fg_ref[...])
    o_ref[...] = y


def _combine(x, slots, route_p, ys, final_g, row0, rows):
    t, d = x.shape
    tm = TOKEN_TILE
    assert row0 % tm == 0 and rows % tm == 0
    b0 = row0 // tm
    final_norm = final_g is not None
    fg = (final_g if final_norm else jnp.ones((d,), F32)).reshape(1, d)
    return pl.pallas_call(
        functools.partial(_combine_kernel, final_norm=final_norm),
        grid=(rows // tm,),
        in_specs=[
            pl.BlockSpec((tm, d), lambda i: (b0 + i, 0)),
            pl.BlockSpec((TOP_K * tm,), lambda i: (b0 + i,), memory_space=pltpu.SMEM),
            pl.BlockSpec((tm, route_p.shape[1]), lambda i: (b0 + i, 0)),
            _const_spec((1, d)),
            pl.BlockSpec(memory_space=pl.ANY),
        ],
        out_specs=pl.BlockSpec((tm, d), lambda i: (i, 0)),
        scratch_shapes=[pltpu.VMEM((TOP_K, tm, d), F32), pltpu.SemaphoreType.DMA(())],
        out_shape=jax.ShapeDtypeStruct((rows, d), F32),
        compiler_params=_params("arbitrary"),
        name="combine",
    )(x, slots, route_p, fg, ys)


def _moe_plan(route_i, counts, n_exp, n_tiles):
    tm = EXPERT_TILE
    cnt = counts[0, :n_exp].astype(jnp.int32)
    tiles = (cnt + tm - 1) // tm
    tile_end = jnp.cumsum(tiles)
    offsets = (tile_end - tiles) * tm
    n_valid = tile_end[-1]
    g = jnp.arange(n_tiles, dtype=jnp.int32)
    gc = jnp.minimum(g, n_valid - 1)
    tile_expert = jnp.sum((gc[:, None] >= tile_end[None, :]).astype(jnp.int32), axis=1)
    slots = offsets[route_i[0:TOP_K]] + route_i[TOP_K:2 * TOP_K]
    slots = slots.reshape(TOP_K, -1, TOKEN_TILE).transpose(1, 0, 2).reshape(-1)
    return slots, tile_end, tile_expert, n_valid.reshape(1)


def _moe_layer(x, g, router_w, wg, wu, wd, final_g, splits):
    t, d = x.shape
    n_exp = router_w.shape[1]
    n_tiles = (t * TOP_K) // EXPERT_TILE + n_exp
    route_i, route_p, counts = _router(x, g, router_w)
    slots, tile_end, tile_expert, n_valid = _moe_plan(route_i, counts, n_exp, n_tiles)
    hs = _dispatch(x, g, slots, tile_end, n_valid, n_tiles)
    ys = _expert_ffn(hs, tile_expert, n_valid, wg, wu, wd)
    return [_combine(x, slots, route_p, ys, final_g, r0, n) for r0, n in splits]


def kernel(x_prompt, x_sample, norm1_g, w_in, conv_w, conv_b, w_fourier, w_conv_out, w_o, norm2_g,
           dense_w_gate, dense_w_up, dense_w_down, router_w, moe_w_gate, moe_w_up, moe_w_down, final_g):
    bp, sp, d = x_prompt.shape
    bs, ss, _ = x_sample.shape
    tp, ts = bp * sp, bs * ss
    t = tp + ts
    depth = norm1_g.shape[0]
    w_f = w_fourier.shape[1]
    gd = w_f // F_GROUPS
    tm = TOKEN_TILE
    assert sp % tm == 0 and ss % tm == 0

    x = jnp.concatenate([x_prompt.reshape(tp, d), x_sample.reshape(ts, d)], axis=0)

    seq_bounds = frozenset([b * sp for b in range(bp + 1)] + [tp + b * ss for b in range(bs + 1)])

    cc, sc = _dft_tables(gd)
    csc = jnp.concatenate([cc, sc], axis=1).astype(BF16)

    for l in range(depth):
        zf, zr = _norm_inproj(x, norm1_g[l], w_in[l].astype(BF16), w_f)
        yf_p = _dft_direct(zf, bp, sp, gd, csc)
        yf_s = _dft_two_stage(zf, tp, bs, ss, gd, csc)
        x = _mixer_out(seq_bounds, yf_p, yf_s, zr, x, conv_w[l], conv_b[l], w_fourier[l].astype(BF16),
                       w_conv_out[l].astype(BF16), w_o[l].astype(BF16))
        last = l == depth - 1
        fg = final_g if last else None
        j = l // 2
        if l % 2 == 0:
            jobs = [] if last else [w[j].reshape(-1, w.shape[-1])
                                    for w in (moe_w_gate, moe_w_up, moe_w_down)]
            x, moe_bf16 = _dense_ffn(x, norm2_g[l], dense_w_gate[j].astype(BF16),
                                     dense_w_up[j].astype(BF16), dense_w_down[j].astype(BF16), fg, jobs)
            if last:
                outs = [x[:tp], x[tp:]]
        else:
            splits = [(0, tp), (tp, ts)] if last else [(0, t)]
            wg, wu, wd = (wb.reshape(w.shape[1:])
                          for wb, w in zip(moe_bf16, (moe_w_gate, moe_w_up, moe_w_down)))
            outs = _moe_layer(x, norm2_g[l], router_w[j], wg, wu, wd, fg, splits)
            if not last:
                x = outs[0]
    return outs[0].reshape(bp, sp, d), outs[1].reshape(bs, ss, d)
```

```python
import functools
import math

import jax
import jax.numpy as jnp
from jax import lax
from jax.experimental import pallas as pl
from jax.experimental.pallas import tpu as pltpu

F_GROUPS = 8
TOP_K = 2
EPS = 1e-6
BF16 = jnp.bfloat16
F32 = jnp.float32

V7X_LANES = 128
V7X_SUBLANES = 8
V7X_VMEM_BYTES = 64 * 1024 * 1024
VMEM_LIMIT = V7X_VMEM_BYTES - 8 * 1024 * 1024

TOKEN_TILE = 512
MIXER_TILE = 256
INPROJ_TILE = 1024
INPROJ_COLS = 1024
FFN_COLS = 512
EXPERT_TILE = 512
EXPERT_COLS = 1024
DMA_ISSUE_UNROLL = 8
DFT_RADIX = 128


def _params(*sem):
    return pltpu.CompilerParams(dimension_semantics=sem, vmem_limit_bytes=VMEM_LIMIT)


def _const_spec(shape):
    return pl.BlockSpec(shape, lambda *_: (0,) * len(shape), pipeline_mode=pl.Buffered(1))


def _rmsnorm(x, g):
    return x * lax.rsqrt(jnp.mean(x * x, axis=-1, keepdims=True) + EPS) * g


def _norm_inproj_kernel(x_ref, g_ref, w_ref, zf_ref, zr_ref, h_ref):
    j = pl.program_id(1)

    @pl.when(j == 0)
    def _():
        h_ref[...] = _rmsnorm(x_ref[...], g_ref[...]).astype(BF16)
        zf_ref[...] = jnp.dot(h_ref[...], w_ref[...], preferred_element_type=F32)

    @pl.when(j > 0)
    def _():
        zr_ref[...] = jnp.dot(h_ref[...], w_ref[...], preferred_element_type=F32).astype(BF16)


def _norm_inproj(x, g, w, w_f):
    t, d = x.shape
    n_in = w.shape[1]
    tm, tn = INPROJ_TILE, INPROJ_COLS
    assert w_f == tn and t % tm == 0 and (n_in - w_f) % tn == 0
    return pl.pallas_call(
        _norm_inproj_kernel,
        grid=(t // tm, n_in // tn),
        in_specs=[
            pl.BlockSpec((tm, d), lambda i, j: (i, 0)),
            pl.BlockSpec((1, d), lambda i, j: (0, 0)),
            pl.BlockSpec((d, tn), lambda i, j: (0, j)),
        ],
        out_specs=[
            pl.BlockSpec((tm, tn), lambda i, j: (i, 0)),
            pl.BlockSpec((tm, tn), lambda i, j: (i, jnp.maximum(j - 1, 0))),
        ],
        out_shape=[
            jax.ShapeDtypeStruct((t, w_f), F32),
            jax.ShapeDtypeStruct((t, n_in - w_f), BF16),
        ],
        scratch_shapes=[pltpu.VMEM((tm, d), BF16)],
        compiler_params=_params("parallel", "arbitrary"),
        name="norm_inproj",
    )(x, g.reshape(1, d), w)


def _dft_tables(n):
    j = lax.broadcasted_iota(jnp.int32, (n, n), 0)
    k = lax.broadcasted_iota(jnp.int32, (n, n), 1)
    ang = ((j * k) % n).astype(F32) * (2.0 * math.pi / n)
    return jnp.cos(ang), jnp.sin(ang)


def _dft_direct_kernel(x_ref, csc_ref, css_ref, o_ref, ab_ref, *, gd):
    s = x_ref.shape[0]

    @pl.when(pl.program_id(1) == 0)
    def _():
        for g in range(x_ref.shape[1] // gd):
            cols = slice(g * gd, (g + 1) * gd)
            ab = jnp.dot(x_ref[:, cols].astype(BF16), csc_ref[...], preferred_element_type=F32)
            ab_ref[0:s, cols] = ab[:, :gd].astype(BF16)
            ab_ref[s:2 * s, cols] = ab[:, gd:].astype(BF16)

    o_ref[...] = jnp.dot(css_ref[...], ab_ref[...], preferred_element_type=F32)


def _dft_direct(zf, nseq, s, gd, csc):
    w_f = zf.shape[1]
    tm = TOKEN_TILE
    cs, sn = _dft_tables(s)
    scale = 1.0 / math.sqrt(s * gd)
    css = (jnp.concatenate([cs, -sn], axis=1) * scale).astype(BF16)
    return pl.pallas_call(
        functools.partial(_dft_direct_kernel, gd=gd),
        grid=(nseq, s // tm),
        in_specs=[
            pl.BlockSpec((s, w_f), lambda b, i: (b, 0)),
            _const_spec((gd, 2 * gd)),
            pl.BlockSpec((tm, 2 * s), lambda b, i: (i, 0)),
        ],
        out_specs=pl.BlockSpec((tm, w_f), lambda b, i: (b * (s // tm) + i, 0)),
        out_shape=jax.ShapeDtypeStruct((nseq * s, w_f), F32),
        scratch_shapes=[pltpu.VMEM((2 * s, w_f), BF16)],
        compiler_params=_params("parallel", "arbitrary"),
        name="dft_direct",
    )(zf, csc, css)


def _dft_stage1_kernel(x_ref, csc_ref, m1_ref, twc_ref, tws_ref, tr_ref, ti_ref,
                       x2, tr2, ti2, xs_ref, ab_ref, *, n1, gd):
    nf = x_ref.shape[1]
    for g in range(x_ref.shape[2] // gd):
        cols = slice(g * gd, (g + 1) * gd)
        x2[...] = x_ref[:, :, cols].reshape(n1 * nf, gd)
        for f in range(nf):
            xs_ref[f * n1:(f + 1) * n1, :] = x2[pl.ds(f, n1, stride=nf), :].astype(BF16)
        ab = jnp.dot(xs_ref[...], csc_ref[...], preferred_element_type=F32)
        for f in range(nf):
            ab_ref[0:n1, f * gd:(f + 1) * gd] = ab[f * n1:(f + 1) * n1, :gd].astype(BF16)
            ab_ref[n1:2 * n1, f * gd:(f + 1) * gd] = ab[f * n1:(f + 1) * n1, gd:].astype(BF16)
        gq = jnp.dot(m1_ref[...], ab_ref[...], preferred_element_type=F32)
        for f in range(nf):
            gr = gq[0:n1, f * gd:(f + 1) * gd]
            gi = gq[n1:2 * n1, f * gd:(f + 1) * gd]
            tc = twc_ref[f]
            ts = tws_ref[f]
            tr2[pl.ds(f, n1, stride=nf), :] = gr * tc + gi * ts
            ti2[pl.ds(f, n1, stride=nf), :] = gi * tc - gr * ts
        tr_ref[:, :, cols] = tr2[...].reshape(n1, nf, gd)
        ti_ref[:, :, cols] = ti2[...].reshape(n1, nf, gd)


def _dft_stage2_kernel(tr_ref, ti_ref, c2s2_ref, o_ref, o2, tcat_ref, *, n2, gd):
    nk = tr_ref.shape[0]
    for g in range(tr_ref.shape[2] // gd):
        cols = slice(g * gd, (g + 1) * gd)
        for q in range(nk):
            tcat_ref[0:n2, q * gd:(q + 1) * gd] = tr_ref[q, :, cols].astype(BF16)
            tcat_ref[n2:2 * n2, q * gd:(q + 1) * gd] = ti_ref[q, :, cols].astype(BF16)
        y = jnp.dot(c2s2_ref[...], tcat_ref[...], preferred_element_type=F32)
        for q in range(nk):
            o2[pl.ds(q, n2, stride=nk), :] = y[:, q * gd:(q + 1) * gd]
        o_ref[:, :, cols] = o2[...].reshape(n2, nk, gd)


def _dft_two_stage(zf, row0, nseq, s, gd, csc):
    w_f = zf.shape[1]
    n1 = DFT_RADIX
    n2 = s // n1
    nf = V7X_SUBLANES
    assert s % n1 == 0 and n2 % nf == 0 and n1 % nf == 0 and gd == V7X_LANES
    if row0 % s == 0:
        blk0 = row0 // s
    else:
        zf, blk0 = zf[row0:], 0
    scale = 1.0 / math.sqrt(s * gd)

    c1, s1 = _dft_tables(n1)
    m1 = jnp.concatenate([jnp.concatenate([c1, -s1], axis=1),
                          jnp.concatenate([-s1, -c1], axis=1)], axis=0).astype(BF16)
    a = lax.broadcasted_iota(jnp.int32, (n2, n1), 0)
    b = lax.broadcasted_iota(jnp.int32, (n2, n1), 1)
    ang = ((a * b) % s).astype(F32) * (2.0 * math.pi / s)
    twc = jnp.broadcast_to((jnp.cos(ang) * scale)[:, :, None], (n2, n1, gd))
    tws = jnp.broadcast_to((jnp.sin(ang) * scale)[:, :, None], (n2, n1, gd))
    c2, s2 = _dft_tables(n2)
    c2s2 = jnp.concatenate([c2, s2], axis=1).astype(BF16)

    x3 = zf.reshape(zf.shape[0] // n2, n2, w_f)
    t_shape = jax.ShapeDtypeStruct((nseq * n1, n2, w_f), F32)
    tr, ti = pl.pallas_call(
        functools.partial(_dft_stage1_kernel, n1=n1, gd=gd),
        grid=(nseq, n2 // nf),
        in_specs=[
            pl.BlockSpec((n1, nf, w_f), lambda b, f: (blk0 + b, f, 0)),
            _const_spec((gd, 2 * gd)),
            _const_spec((2 * n1, 2 * n1)),
            pl.BlockSpec((nf, n1, gd), lambda b, f: (f, 0, 0)),
            pl.BlockSpec((nf, n1, gd), lambda b, f: (f, 0, 0)),
        ],
        out_specs=[
            pl.BlockSpec((n1, nf, w_f), lambda b, f: (b, f, 0)),
            pl.BlockSpec((n1, nf, w_f), lambda b, f: (b, f, 0)),
        ],
        out_shape=[t_shape, t_shape],
        scratch_shapes=[pltpu.VMEM((n1 * nf, gd), F32), pltpu.VMEM((n1 * nf, gd), F32),
                        pltpu.VMEM((n1 * nf, gd), F32),
                        pltpu.VMEM((nf * n1, gd), BF16), pltpu.VMEM((2 * n1, nf * gd), BF16)],
        compiler_params=_params("parallel", "parallel"),
        name="dft_stage1",
    )(x3, csc, m1, twc, tws)

    y = pl.pallas_call(
        functools.partial(_dft_stage2_kernel, n2=n2, gd=gd),
        grid=(nseq, n1 // nf),
        in_specs=[
            pl.BlockSpec((nf, n2, w_f), lambda b, q: (b * (n1 // nf) + q, 0, 0)),
            pl.BlockSpec((nf, n2, w_f), lambda b, q: (b * (n1 // nf) + q, 0, 0)),
            _const_spec((n2, 2 * n2)),
        ],
        out_specs=pl.BlockSpec((n2, nf, w_f), lambda b, q: (b, q, 0)),
        out_shape=jax.ShapeDtypeStruct((nseq * n2, n1, w_f), F32),
        scratch_shapes=[pltpu.VMEM((n2 * nf, gd), F32), pltpu.VMEM((2 * n2, nf * gd), BF16)],
        compiler_params=_params("parallel", "parallel"),
        name="dft_stage2",
    )(tr, ti, c2s2)
    return y.reshape(nseq * s, w_f)


def _mixer_out_kernel(edge_ref, yfa_ref, yfb_ref, zr_ref, pc_ref, pv_ref, nc_ref, nv_ref, x_ref,
                      cw_ref, cb_ref, wf_ref, wc_ref, wo_ref, o_ref, *, n_a, w_c, d):
    i = pl.program_id(0)
    tm = x_ref.shape[0]
    halo = pc_ref.shape[0]
    zb = zr_ref[:, 0:w_c].astype(F32)
    u = zr_ref[:, w_c:2 * w_c].astype(F32) * zr_ref[:, 2 * w_c:3 * w_c].astype(F32)
    u_prev = pc_ref[halo - 1:halo, :].astype(F32) * pv_ref[halo - 1:halo, :].astype(F32)
    u_next = nc_ref[0:1, :].astype(F32) * nv_ref[0:1, :].astype(F32)
    u_prev = jnp.where(edge_ref[i, 0] > 0, u_prev, 0.0)
    u_next = jnp.where(edge_ref[i, 1] > 0, u_next, 0.0)
    row = lax.broadcasted_iota(jnp.int32, (tm, w_c), 0)
    u_m1 = jnp.where(row == 0, u_prev, pltpu.roll(u, 1, 0))
    u_p1 = jnp.where(row == tm - 1, u_next, pltpu.roll(u, tm - 1, 0))
    cv = u_m1 * cw_ref[0:1, :] + u * cw_ref[1:2, :] + u_p1 * cw_ref[2:3, :] + cb_ref[...]
    yc = (zb * cv).astype(BF16)
    yf = jnp.where(i < n_a, yfa_ref[...], yfb_ref[...]).astype(BF16)
    gf = jax.nn.sigmoid(zr_ref[:, 3 * w_c:3 * w_c + d].astype(F32))
    gc = jax.nn.sigmoid(zr_ref[:, 3 * w_c + d:3 * w_c + 2 * d].astype(F32))
    m = (gf * jnp.dot(yf, wf_ref[...], preferred_element_type=F32)
         + gc * jnp.dot(yc, wc_ref[...], preferred_element_type=F32))
    o_ref[...] = x_ref[...] + jnp.dot(m.astype(BF16), wo_ref[...], preferred_element_type=F32)


def _mixer_out(seq_bounds, yf_a, yf_b, zr, x, conv_w, conv_b, w_fourier, w_conv_out, w_o):
    t, d = x.shape
    w_f = w_fourier.shape[0]
    w_c = w_conv_out.shape[0]
    tm = MIXER_TILE
    halo = 2 * V7X_SUBLANES
    n_a = yf_a.shape[0] // tm
    n_b = yf_b.shape[0] // tm
    nh = t // halo
    hb = tm // halo
    assert w_c % V7X_LANES == 0 and t % tm == 0 and all(b % tm == 0 for b in seq_bounds)
    edges = jnp.asarray([[0 if i * tm in seq_bounds else 1, 0 if (i + 1) * tm in seq_bounds else 1]
                         for i in range(t // tm)], dtype=jnp.int32)
    return pl.pallas_call(
        functools.partial(_mixer_out_kernel, n_a=n_a, w_c=w_c, d=d),
        grid_spec=pltpu.PrefetchScalarGridSpec(
            num_scalar_prefetch=1,
            grid=(t // tm,),
            in_specs=[
                pl.BlockSpec((tm, w_f), lambda i, e: (jnp.minimum(i, n_a - 1), 0)),
                pl.BlockSpec((tm, w_f), lambda i, e: (jnp.clip(i - n_a, 0, n_b - 1), 0)),
                pl.BlockSpec((tm, zr.shape[1]), lambda i, e: (i, 0)),
                pl.BlockSpec((halo, w_c), lambda i, e: (jnp.maximum(i * hb - 1, 0), 1)),
                pl.BlockSpec((halo, w_c), lambda i, e: (jnp.maximum(i * hb - 1, 0), 2)),
                pl.BlockSpec((halo, w_c), lambda i, e: (jnp.minimum((i + 1) * hb, nh - 1), 1)),
                pl.BlockSpec((halo, w_c), lambda i, e: (jnp.minimum((i + 1) * hb, nh - 1), 2)),
                pl.BlockSpec((tm, d), lambda i, e: (i, 0)),
                _const_spec((conv_w.shape[0], w_c)),
                _const_spec((1, w_c)),
                _const_spec((w_f, d)),
                _const_spec((w_c, d)),
                _const_spec((d, d)),
            ],
            out_specs=pl.BlockSpec((tm, d), lambda i, e: (i, 0)),
        ),
        out_shape=jax.ShapeDtypeStruct((t, d), F32),
        compiler_params=_params("parallel"),
        name="mixer_out",
    )(edges, yf_a, yf_b, zr, zr, zr, zr, zr, x, conv_w, conv_b.reshape(1, w_c),
      w_fourier, w_conv_out, w_o)


def _dense_ffn_kernel(*refs, final_norm, n_cast):
    x_ref, g_ref, wg_ref, wu_ref, wd_ref, fg_ref = refs[:6]
    cast_in = refs[6:6 + n_cast]
    o_ref = refs[6 + n_cast]
    cast_out = refs[7 + n_cast:7 + 2 * n_cast]
    h_ref = refs[7 + 2 * n_cast]
    j = pl.program_id(1)

    @pl.when(j == 0)
    def _():
        h_ref[...] = _rmsnorm(x_ref[...], g_ref[...]).astype(BF16)
        o_ref[...] = x_ref[...]

    h = h_ref[...]
    a = jnp.dot(h, wg_ref[...], preferred_element_type=F32)
    b = jnp.dot(h, wu_ref[...], preferred_element_type=F32)
    act = (a * jax.nn.sigmoid(a) * b).astype(BF16)
    o_ref[...] += jnp.dot(act, wd_ref[...], preferred_element_type=F32)

    if final_norm:
        @pl.when(j == pl.num_programs(1) - 1)
        def _():
            o_ref[...] = _rmsnorm(o_ref[...], fg_ref[...])

    for src, dst in zip(cast_in, cast_out):
        dst[...] = src[...].astype(BF16)


def _cast_chunk_rows(rows, n_steps):
    step = 2 * V7X_SUBLANES
    chunk = step
    while rows % chunk or rows // chunk > n_steps:
        chunk += step
        assert chunk <= rows
    return chunk


def _dense_ffn(x, g, wg, wu, wd, final_g, cast_jobs=()):
    t, d = x.shape
    f = wg.shape[1]
    tm, tf = TOKEN_TILE, FFN_COLS
    assert t % tm == 0 and f % tf == 0
    nj = f // tf
    n_steps = (t // tm) * nj
    final_norm = final_g is not None
    fg = (final_g if final_norm else jnp.ones((d,), F32)).reshape(1, d)

    def cast_specs():
        specs = []
        for w in cast_jobs:
            rows, cols = w.shape
            chunk = _cast_chunk_rows(rows, n_steps)
            last = rows // chunk - 1
            specs.append(pl.BlockSpec(
                (chunk, cols), lambda i, j, last=last: (jnp.minimum(i * nj + j, last), 0)))
        return specs

    outs = pl.pallas_call(
        functools.partial(_dense_ffn_kernel, final_norm=final_norm, n_cast=len(cast_jobs)),
        grid=(t // tm, nj),
        in_specs=[
            pl.BlockSpec((tm, d), lambda i, j: (i, 0)),
            pl.BlockSpec((1, d), lambda i, j: (0, 0)),
            pl.BlockSpec((d, tf), lambda i, j: (0, j)),
            pl.BlockSpec((d, tf), lambda i, j: (0, j)),
            pl.BlockSpec((tf, d), lambda i, j: (j, 0)),
            pl.BlockSpec((1, d), lambda i, j: (0, 0)),
        ] + cast_specs(),
        out_specs=[pl.BlockSpec((tm, d), lambda i, j: (i, 0))] + cast_specs(),
        out_shape=[jax.ShapeDtypeStruct((t, d), F32)]
        + [jax.ShapeDtypeStruct(w.shape, BF16) for w in cast_jobs],
        scratch_shapes=[pltpu.VMEM((tm, d), BF16)],
        compiler_params=_params("arbitrary", "arbitrary"),
        name="dense_ffn",
    )(x, g.reshape(1, d), wg, wu, wd, fg, *cast_jobs)
    return outs[0], list(outs[1:])


def _router_kernel(x_ref, g_ref, rwh_ref, rwl_ref, tri_ref, ri_ref, rp_ref, cnt_ref, carry_ref, *, n_exp):
    @pl.when(pl.program_id(0) == 0)
    def _():
        carry_ref[...] = jnp.zeros_like(carry_ref)

    tm = x_ref.shape[0]
    lanes = rwh_ref.shape[1]
    h = _rmsnorm(x_ref[...], g_ref[...])
    h_hi = h.astype(BF16)
    h_lo = (h - h_hi.astype(F32)).astype(BF16)
    logits = (jnp.dot(h_hi, rwh_ref[...], preferred_element_type=F32)
              + jnp.dot(h_lo, rwh_ref[...], preferred_element_type=F32)
              + jnp.dot(h_hi, rwl_ref[...], preferred_element_type=F32))
    lane = lax.broadcasted_iota(jnp.int32, (tm, lanes), 1).astype(F32)
    neg = jnp.float32(-jnp.inf)
    lg = jnp.where(lane < n_exp, logits, neg)
    m1 = jnp.max(lg, axis=1, keepdims=True)
    i1 = jnp.min(jnp.where(lg == m1, lane, float(lanes)), axis=1, keepdims=True)
    lg2 = jnp.where(lane == i1, neg, lg)
    m2 = jnp.max(lg2, axis=1, keepdims=True)
    i2 = jnp.min(jnp.where(lg2 == m2, lane, float(lanes)), axis=1, keepdims=True)
    sel = jnp.where((lane == i1) | (lane == i2), 1.0, 0.0)
    before = jnp.dot(tri_ref[...], sel.astype(BF16), preferred_element_type=F32)
    rank = carry_ref[...] + before
    r1 = jnp.sum(jnp.where(lane == i1, rank, 0.0), axis=1, keepdims=True)
    r2 = jnp.sum(jnp.where(lane == i2, rank, 0.0), axis=1, keepdims=True)
    carry_ref[...] += jnp.sum(sel, axis=0, keepdims=True)
    cnt_ref[...] = carry_ref[...]
    e21 = jnp.exp(m2 - m1)
    p1 = 1.0 / (1.0 + e21)
    p2 = e21 / (1.0 + e21)
    rp_ref[...] = jnp.where(lane == 0, p1, jnp.where(lane == 1, p2, 0.0))
    q = jnp.where(lane == 0, i1, jnp.where(lane == 1, i2, jnp.where(lane == 2, r1, jnp.where(lane == 3, r2, 0.0))))
    ri_ref[...] = q.T[0:ri_ref.shape[0], :].astype(jnp.int32)


def _router(x, g, router_w):
    t, d = x.shape
    n_exp = router_w.shape[1]
    tm = TOKEN_TILE
    lanes = V7X_LANES
    rw = jnp.zeros((d, lanes), F32).at[:, :n_exp].set(router_w)
    rw_hi = rw.astype(BF16)
    rw_lo = (rw - rw_hi.astype(F32)).astype(BF16)
    tri =(lax.broadcasted_iota(jnp.int32, (tm, tm), 1)
           < lax.broadcasted_iota(jnp.int32, (tm, tm), 0)).astype(BF16)
    return pl.pallas_call(
        functools.partial(_router_kernel, n_exp=n_exp),
        grid=(t // tm,),
        in_specs=[
            pl.BlockSpec((tm, d), lambda i: (i, 0)),
            _const_spec((1, d)),
            _const_spec((d, lanes)),
            _const_spec((d, lanes)),
            _const_spec((tm, tm)),
        ],
        out_specs=[
            pl.BlockSpec((V7X_SUBLANES, tm), lambda i: (0, i)),
            pl.BlockSpec((tm, lanes), lambda i: (i, 0)),
            pl.BlockSpec((1, lanes), lambda i: (0, 0)),
        ],
        out_shape=[
            jax.ShapeDtypeStruct((V7X_SUBLANES, t), jnp.int32),
            jax.ShapeDtypeStruct((t, lanes), F32),
            jax.ShapeDtypeStruct((1, lanes), F32),
        ],
        scratch_shapes=[pltpu.VMEM((1, lanes), F32)],
        compiler_params=_params("arbitrary"),
        name="router",
    )(x, g.reshape(1, d), rw_hi, rw_lo, tri)


def _row_copy(src_ref, src_row, dst_ref, dst_row, sem):
    return pltpu.make_async_copy(src_ref.at[pl.ds(src_row, 1), :], dst_ref.at[pl.ds(dst_row, 1), :], sem)


def _dispatch_kernel(tend_ref, nv_ref, x_ref, g_ref, slot_ref, hs_ref, h_ref, sem, zsem, *, n_tiles):
    tm = x_ref.shape[0]

    @pl.when(pl.program_id(0) == 0)
    def _():
        h_ref[...] = jnp.zeros_like(h_ref)

        def zero_tile(tile):
            cp = pltpu.make_async_copy(h_ref, hs_ref.at[pl.ds(tile * tm, tm), :], zsem)
            cp.start()
            cp.wait()

        for e in range(tend_ref.shape[0]):
            first = tend_ref[e - 1] if e else 0

            @pl.when(tend_ref[e] > first)
            def _():
                zero_tile(tend_ref[e] - 1)

        def tail(tile, c):
            zero_tile(tile)
            return c

        lax.fori_loop(nv_ref[0], n_tiles, tail, 0)

    h_ref[...] = _rmsnorm(x_ref[...], g_ref[...])

    def issue(s, c):
        for k in range(TOP_K):
            _row_copy(h_ref, s, hs_ref, slot_ref[k * tm + s], sem).start()
        return c

    lax.fori_loop(0, tm, issue, 0, unroll=DMA_ISSUE_UNROLL)
    for _ in range(TOP_K):
        pltpu.make_async_copy(h_ref, hs_ref.at[pl.ds(0, tm), :], sem).wait()


def _dispatch(x, g, slots, tile_end, n_valid, n_tiles):
    t, d = x.shape
    tm = TOKEN_TILE
    assert tm == EXPERT_TILE
    return pl.pallas_call(
        functools.partial(_dispatch_kernel, n_tiles=n_tiles),
        grid_spec=pltpu.PrefetchScalarGridSpec(
            num_scalar_prefetch=2,
            grid=(t // tm,),
            in_specs=[
                pl.BlockSpec((tm, d), lambda i, *_: (i, 0)),
                _const_spec((1, d)),
                pl.BlockSpec((TOP_K * tm,), lambda i, *_: (i,), memory_space=pltpu.SMEM),
            ],
            out_specs=pl.BlockSpec(memory_space=pl.ANY),
            scratch_shapes=[pltpu.VMEM((tm, d), F32), pltpu.SemaphoreType.DMA(()),
                            pltpu.SemaphoreType.DMA(())],
        ),
        out_shape=jax.ShapeDtypeStruct((n_tiles * EXPERT_TILE, d), F32),
        compiler_params=_params("arbitrary"),
        name="dispatch",
    )(tile_end, n_valid, x, g.reshape(1, d), slots)


def _expert_ffn_kernel(te_ref, nv_ref, hs_ref, wg_ref, wu_ref, wd_ref, o_ref, xb_ref):
    del te_ref
    j = pl.program_id(1)
    used = pl.program_id(0) < nv_ref[0]

    @pl.when(jnp.logical_and(jnp.logical_not(used), j == 0))
    def _():
        o_ref[...] = jnp.zeros_like(o_ref)

    @pl.when(used)
    def _():
        @pl.when(j == 0)
        def _():
            xb_ref[...] = hs_ref[...].astype(BF16)
            o_ref[...] = jnp.zeros_like(o_ref)

        xb = xb_ref[...]
        a = jnp.dot(xb, wg_ref[...], preferred_element_type=F32)
        b = jnp.dot(xb, wu_ref[...], preferred_element_type=F32)
        act = (a * jax.nn.sigmoid(a) * b).astype(BF16)
        o_ref[...] += jnp.dot(act, wd_ref[...], preferred_element_type=F32)


def _expert_ffn(hs, tile_expert, n_valid, wg, wu, wd):
    n_rows, d = hs.shape
    f = wg.shape[2]
    tm, tf = EXPERT_TILE, EXPERT_COLS
    assert n_rows % tm == 0 and f % tf == 0
    nj = f // tf

    def col(g, j, nv):
        return jnp.where(g < nv[0], j, nj - 1)

    return pl.pallas_call(
        _expert_ffn_kernel,
        grid_spec=pltpu.PrefetchScalarGridSpec(
            num_scalar_prefetch=2,
            grid=(n_rows // tm, nj),
            in_specs=[
                pl.BlockSpec((tm, d), lambda g, j, te, nv: (g, 0)),
                pl.BlockSpec((None, d, tf), lambda g, j, te, nv: (te[g], 0, col(g, j, nv))),
                pl.BlockSpec((None, d, tf), lambda g, j, te, nv: (te[g], 0, col(g, j, nv))),
                pl.BlockSpec((None, tf, d), lambda g, j, te, nv: (te[g], col(g, j, nv), 0)),
            ],
            out_specs=pl.BlockSpec((tm, d), lambda g, j, te, nv: (g, 0)),
            scratch_shapes=[pltpu.VMEM((tm, d), BF16)],
        ),
        out_shape=jax.ShapeDtypeStruct((n_rows, d), F32),
        compiler_params=_params("arbitrary", "arbitrary"),
        name="expert_ffn",
    )(tile_expert, n_valid, hs, wg, wu, wd)


def _combine_kernel(x_ref, slot_ref, rp_ref, fg_ref, ys_ref, o_ref, y_ref, sem, *, final_norm):
    tm = x_ref.shape[0]

    def issue(s, c):
        for k in range(TOP_K):
            _row_copy(ys_ref, slot_ref[k * tm + s], y_ref.at[k], s, sem).start()
        return c

    lax.fori_loop(0, tm, issue, 0, unroll=DMA_ISSUE_UNROLL)
    for k in range(TOP_K):
        pltpu.make_async_copy(ys_ref.at[pl.ds(0, tm), :], y_ref.at[k], sem).wait()

    y = x_ref[...] + rp_ref[:, 0:1] * y_ref[0] + rp_ref[:, 1:2] * y_ref[1]
    if final_norm:
        y = _rmsnorm(y, fg_ref[...])
    o_ref[...] = y


def _combine(x, slots, route_p, ys, final_g, row0, rows):
    t, d = x.shape
    tm = TOKEN_TILE
    assert row0 % tm == 0 and rows % tm == 0
    b0 = row0 // tm
    final_norm = final_g is not None
    fg = (final_g if final_norm else jnp.ones((d,), F32)).reshape(1, d)
    return pl.pallas_call(
        functools.partial(_combine_kernel, final_norm=final_norm),
        grid=(rows // tm,),
        in_specs=[
            pl.BlockSpec((tm, d), lambda i: (b0 + i, 0)),
            pl.BlockSpec((TOP_K * tm,), lambda i: (b0 + i,), memory_space=pltpu.SMEM),
            pl.BlockSpec((tm, route_p.shape[1]), lambda i: (b0 + i, 0)),
            _const_spec((1, d)),
            pl.BlockSpec(memory_space=pl.ANY),
        ],
        out_specs=pl.BlockSpec((tm, d), lambda i: (i, 0)),
        scratch_shapes=[pltpu.VMEM((TOP_K, tm, d), F32), pltpu.SemaphoreType.DMA(())],
        out_shape=jax.ShapeDtypeStruct((rows, d), F32),
        compiler_params=_params("arbitrary"),
        name="combine",
    )(x, slots, route_p, fg, ys)


def _moe_plan(route_i, counts, n_exp, n_tiles):
    tm = EXPERT_TILE
    cnt = counts[0, :n_exp].astype(jnp.int32)
    tiles = (cnt + tm - 1) // tm
    tile_end = jnp.cumsum(tiles)
    offsets = (tile_end - tiles) * tm
    n_valid = tile_end[-1]
    g = jnp.arange(n_tiles, dtype=jnp.int32)
    gc = jnp.minimum(g, n_valid - 1)
    tile_expert = jnp.sum((gc[:, None] >= tile_end[None, :]).astype(jnp.int32), axis=1)
    expert_ids = jnp.arange(n_exp, dtype=jnp.int32)[None, :, None]
    chosen = route_i[0:TOP_K, None, :] == expert_ids
    slots = jnp.sum(jnp.where(chosen, offsets[None, :, None], 0), axis=1) + route_i[TOP_K:2 * TOP_K]
    slots = slots.reshape(TOP_K, -1, TOKEN_TILE).transpose(1, 0, 2).reshape(-1)
    return slots, tile_end, tile_expert, n_valid.reshape(1)


def _moe_layer(x, g, router_w, wg, wu, wd, final_g, splits):
    t, d = x.shape
    n_exp = router_w.shape[1]
    n_tiles = (t * TOP_K) // EXPERT_TILE + n_exp
    route_i, route_p, counts = _router(x, g, router_w)
    slots, tile_end, tile_expert, n_valid = _moe_plan(route_i, counts, n_exp, n_tiles)
    hs = _dispatch(x, g, slots, tile_end, n_valid, n_tiles)
    ys = _expert_ffn(hs, tile_expert, n_valid, wg, wu, wd)
    return [_combine(x, slots, route_p, ys, final_g, r0, n) for r0, n in splits]


def kernel(x_prompt, x_sample, norm1_g, w_in, conv_w, conv_b, w_fourier, w_conv_out, w_o, norm2_g,
           dense_w_gate, dense_w_up, dense_w_down, router_w, moe_w_gate, moe_w_up, moe_w_down, final_g):
    bp, sp, d = x_prompt.shape
    bs, ss, _ = x_sample.shape
    tp, ts = bp * sp, bs * ss
    t = tp + ts
    depth = norm1_g.shape[0]
    w_f = w_fourier.shape[1]
    gd = w_f // F_GROUPS
    tm = TOKEN_TILE
    assert sp % tm == 0 and ss % tm == 0

    x = jnp.concatenate([x_prompt.reshape(tp, d), x_sample.reshape(ts, d)], axis=0)

    seq_bounds = frozenset([b * sp for b in range(bp + 1)] + [tp + b * ss for b in range(bs + 1)])

    cc, sc = _dft_tables(gd)
    csc = jnp.concatenate([cc, sc], axis=1).astype(BF16)

    for l in range(depth):
        zf, zr = _norm_inproj(x, norm1_g[l], w_in[l].astype(BF16), w_f)
        yf_p = _dft_direct(zf, bp, sp, gd, csc)
        yf_s = _dft_two_stage(zf, tp, bs, ss, gd, csc)
        x = _mixer_out(seq_bounds, yf_p, yf_s, zr, x, conv_w[l], conv_b[l], w_fourier[l].astype(BF16),
                       w_conv_out[l].astype(BF16), w_o[l].astype(BF16))
        last = l == depth - 1
        fg = final_g if last else None
        j = l // 2
        if l % 2 == 0:
            jobs = [] if last else [w[j].reshape(-1, w.shape[-1])
                                    for w in (moe_w_gate, moe_w_up, moe_w_down)]
            x, moe_bf16 = _dense_ffn(x, norm2_g[l], dense_w_gate[j].astype(BF16),
                                     dense_w_up[j].astype(BF16), dense_w_down[j].astype(BF16), fg, jobs)
            if last:
                outs = [x[:tp], x[tp:]]
        else:
            splits = [(0, tp), (tp, ts)] if last else [(0, t)]
            wg, wu, wd = (wb.reshape(w.shape[1:])
                          for wb, w in zip(moe_bf16, (moe_w_gate, moe_w_up, moe_w_down)))
            outs = _moe_layer(x, norm2_g[l], router_w[j], wg, wu, wd, fg, splits)
            if not last:
                x = outs[0]
    return outs[0].reshape(bp, sp, d), outs[1].reshape(bs, ss, d)
```

```python
import functools
import math

import jax
import jax.numpy as jnp
from jax import lax
from jax.experimental import pallas as pl
from jax.experimental.pallas import tpu as pltpu

F_GROUPS = 8
TOP_K = 2
EPS = 1e-6
BF16 = jnp.bfloat16
F32 = jnp.float32

V7X_LANES = 128
V7X_SUBLANES = 8
V7X_VMEM_BYTES = 64 * 1024 * 1024
VMEM_LIMIT = V7X_VMEM_BYTES - 8 * 1024 * 1024

TOKEN_TILE = 512
MIXER_TILE = 256
INPROJ_TILE = 1024
INPROJ_COLS = 1024
FFN_COLS = 512
EXPERT_TILE = 512
EXPERT_COLS = 1024
DMA_ISSUE_UNROLL = 8
DFT_RADIX = 128


def _params(*sem):
    return pltpu.CompilerParams(dimension_semantics=sem, vmem_limit_bytes=VMEM_LIMIT)


def _const_spec(shape):
    return pl.BlockSpec(shape, lambda *_: (0,) * len(shape), pipeline_mode=pl.Buffered(1))


def _rmsnorm(x, g):
    return x * lax.rsqrt(jnp.mean(x * x, axis=-1, keepdims=True) + EPS) * g


def _norm_inproj_kernel(*refs, part_tiles):
    n_parts = len(part_tiles)
    x_parts = refs[:n_parts]
    g_ref, w_ref, zf_ref, zr_ref, h_ref, xbuf, sem = refs[n_parts:]
    i = pl.program_id(0)
    j = pl.program_id(1)
    tm = xbuf.shape[1]

    def tile_copy(part, row, slot):
        return pltpu.make_async_copy(part.at[pl.ds(row, tm), :], xbuf.at[slot], sem.at[slot])

    def start_fetch(tile, slot):
        first = 0
        for part, n_tiles in zip(x_parts, part_tiles):
            @pl.when(jnp.logical_and(tile >= first, tile < first + n_tiles))
            def _(part=part, first=first):
                tile_copy(part, (tile - first) * tm, slot).start()
            first += n_tiles

    @pl.when(j == 0)
    def _():
        slot = i % 2

        @pl.when(i == 0)
        def _():
            start_fetch(i, slot)

        tile_copy(x_parts[0], 0, slot).wait()

        @pl.when(i + 1 < pl.num_programs(0))
        def _():
            start_fetch(i + 1, 1 - slot)

        h_ref[...] = _rmsnorm(xbuf[slot], g_ref[...]).astype(BF16)
        zf_ref[...] = jnp.dot(h_ref[...], w_ref[...], preferred_element_type=F32)

    @pl.when(j > 0)
    def _():
        zr_ref[...] = jnp.dot(h_ref[...], w_ref[...], preferred_element_type=F32).astype(BF16)


def _norm_inproj(x_parts, g, w, w_f):
    d = x_parts[0].shape[1]
    t = sum(p.shape[0] for p in x_parts)
    n_in = w.shape[1]
    tm, tn = INPROJ_TILE, INPROJ_COLS
    assert w_f == tn and (n_in - w_f) % tn == 0 and all(p.shape[0] % tm == 0 for p in x_parts)
    part_tiles = tuple(p.shape[0] // tm for p in x_parts)
    return pl.pallas_call(
        functools.partial(_norm_inproj_kernel, part_tiles=part_tiles),
        grid=(t // tm, n_in // tn),
        in_specs=[pl.BlockSpec(memory_space=pl.ANY) for _ in x_parts] + [
            pl.BlockSpec((1, d), lambda i, j: (0, 0)),
            pl.BlockSpec((d, tn), lambda i, j: (0, j)),
        ],
        out_specs=[
            pl.BlockSpec((tm, tn), lambda i, j: (i, 0)),
            pl.BlockSpec((tm, tn), lambda i, j: (i, jnp.maximum(j - 1, 0))),
        ],
        out_shape=[
            jax.ShapeDtypeStruct((t, w_f), F32),
            jax.ShapeDtypeStruct((t, n_in - w_f), BF16),
        ],
        scratch_shapes=[pltpu.VMEM((tm, d), BF16), pltpu.VMEM((2, tm, d), F32),
                        pltpu.SemaphoreType.DMA((2,))],
        compiler_params=_params("arbitrary", "arbitrary"),
        name="norm_inproj",
    )(*x_parts, g.reshape(1, d), w)


def _dft_tables(n):
    j = lax.broadcasted_iota(jnp.int32, (n, n), 0)
    k = lax.broadcasted_iota(jnp.int32, (n, n), 1)
    ang = ((j * k) % n).astype(F32) * (2.0 * math.pi / n)
    return jnp.cos(ang), jnp.sin(ang)


def _dft_direct_kernel(x_ref, csc_ref, css_ref, o_ref, ab_ref, *, gd):
    s = x_ref.shape[0]

    @pl.when(pl.program_id(1) == 0)
    def _():
        for g in range(x_ref.shape[1] // gd):
            cols = slice(g * gd, (g + 1) * gd)
            ab = jnp.dot(x_ref[:, cols].astype(BF16), csc_ref[...], preferred_element_type=F32)
            ab_ref[0:s, cols] = ab[:, :gd].astype(BF16)
            ab_ref[s:2 * s, cols] = ab[:, gd:].astype(BF16)

    o_ref[...] = jnp.dot(css_ref[...], ab_ref[...], preferred_element_type=F32)


def _dft_direct(zf, nseq, s, gd, csc):
    w_f = zf.shape[1]
    tm = TOKEN_TILE
    cs, sn = _dft_tables(s)
    scale = 1.0 / math.sqrt(s * gd)
    css = (jnp.concatenate([cs, -sn], axis=1) * scale).astype(BF16)
    return pl.pallas_call(
        functools.partial(_dft_direct_kernel, gd=gd),
        grid=(nseq, s // tm),
        in_specs=[
            pl.BlockSpec((s, w_f), lambda b, i: (b, 0)),
            _const_spec((gd, 2 * gd)),
            pl.BlockSpec((tm, 2 * s), lambda b, i: (i, 0)),
        ],
        out_specs=pl.BlockSpec((tm, w_f), lambda b, i: (b * (s // tm) + i, 0)),
        out_shape=jax.ShapeDtypeStruct((nseq * s, w_f), F32),
        scratch_shapes=[pltpu.VMEM((2 * s, w_f), BF16)],
        compiler_params=_params("parallel", "arbitrary"),
        name="dft_direct",
    )(zf, csc, css)


def _dft_stage1_kernel(x_ref, csc_ref, m1_ref, twc_ref, tws_ref, tr_ref, ti_ref,
                       x2, tr2, ti2, xs_ref, ab_ref, *, n1, gd):
    nf = x_ref.shape[1]
    for g in range(x_ref.shape[2] // gd):
        cols = slice(g * gd, (g + 1) * gd)
        x2[...] = x_ref[:, :, cols].reshape(n1 * nf, gd)
        for f in range(nf):
            xs_ref[f * n1:(f + 1) * n1, :] = x2[pl.ds(f, n1, stride=nf), :].astype(BF16)
        ab = jnp.dot(xs_ref[...], csc_ref[...], preferred_element_type=F32)
        for f in range(nf):
            ab_ref[0:n1, f * gd:(f + 1) * gd] = ab[f * n1:(f + 1) * n1, :gd].astype(BF16)
            ab_ref[n1:2 * n1, f * gd:(f + 1) * gd] = ab[f * n1:(f + 1) * n1, gd:].astype(BF16)
        gq = jnp.dot(m1_ref[...], ab_ref[...], preferred_element_type=F32)
        for f in range(nf):
            gr = gq[0:n1, f * gd:(f + 1) * gd]
            gi = gq[n1:2 * n1, f * gd:(f + 1) * gd]
            tc = twc_ref[f]
            ts = tws_ref[f]
            tr2[pl.ds(f, n1, stride=nf), :] = gr * tc + gi * ts
            ti2[pl.ds(f, n1, stride=nf), :] = gi * tc - gr * ts
        tr_ref[:, :, cols] = tr2[...].reshape(n1, nf, gd)
        ti_ref[:, :, cols] = ti2[...].reshape(n1, nf, gd)


def _dft_stage2_kernel(tr_ref, ti_ref, c2s2_ref, o_ref, o2, tcat_ref, *, n2, gd):
    nk = tr_ref.shape[0]
    for g in range(tr_ref.shape[2] // gd):
        cols = slice(g * gd, (g + 1) * gd)
        for q in range(nk):
            tcat_ref[0:n2, q * gd:(q + 1) * gd] = tr_ref[q, :, cols].astype(BF16)
            tcat_ref[n2:2 * n2, q * gd:(q + 1) * gd] = ti_ref[q, :, cols].astype(BF16)
        y = jnp.dot(c2s2_ref[...], tcat_ref[...], preferred_element_type=F32)
        for q in range(nk):
            o2[pl.ds(q, n2, stride=nk), :] = y[:, q * gd:(q + 1) * gd]
        o_ref[:, :, cols] = o2[...].reshape(n2, nk, gd)


def _dft_two_stage(zf, row0, nseq, s, gd, csc):
    w_f = zf.shape[1]
    n1 = DFT_RADIX
    n2 = s // n1
    nf = V7X_SUBLANES
    assert s % n1 == 0 and n2 % nf == 0 and n1 % nf == 0 and gd == V7X_LANES
    if row0 % s == 0:
        blk0 = row0 // s
    else:
        zf, blk0 = zf[row0:], 0
    scale = 1.0 / math.sqrt(s * gd)

    c1, s1 = _dft_tables(n1)
    m1 = jnp.concatenate([jnp.concatenate([c1, -s1], axis=1),
                          jnp.concatenate([-s1, -c1], axis=1)], axis=0).astype(BF16)
    a = lax.broadcasted_iota(jnp.int32, (n2, n1), 0)
    b = lax.broadcasted_iota(jnp.int32, (n2, n1), 1)
    ang = ((a * b) % s).astype(F32) * (2.0 * math.pi / s)
    twc = jnp.broadcast_to((jnp.cos(ang) * scale)[:, :, None], (n2, n1, gd))
    tws = jnp.broadcast_to((jnp.sin(ang) * scale)[:, :, None], (n2, n1, gd))
    c2, s2 = _dft_tables(n2)
    c2s2 = jnp.concatenate([c2, s2], axis=1).astype(BF16)

    x3 = zf.reshape(zf.shape[0] // n2, n2, w_f)
    t_shape = jax.ShapeDtypeStruct((nseq * n1, n2, w_f), F32)
    tr, ti = pl.pallas_call(
        functools.partial(_dft_stage1_kernel, n1=n1, gd=gd),
        grid=(nseq, n2 // nf),
        in_specs=[
            pl.BlockSpec((n1, nf, w_f), lambda b, f: (blk0 + b, f, 0)),
            _const_spec((gd, 2 * gd)),
            _const_spec((2 * n1, 2 * n1)),
            pl.BlockSpec((nf, n1, gd), lambda b, f: (f, 0, 0)),
            pl.BlockSpec((nf, n1, gd), lambda b, f: (f, 0, 0)),
        ],
        out_specs=[
            pl.BlockSpec((n1, nf, w_f), lambda b, f: (b, f, 0)),
            pl.BlockSpec((n1, nf, w_f), lambda b, f: (b, f, 0)),
        ],
        out_shape=[t_shape, t_shape],
        scratch_shapes=[pltpu.VMEM((n1 * nf, gd), F32), pltpu.VMEM((n1 * nf, gd), F32),
                        pltpu.VMEM((n1 * nf, gd), F32),
                        pltpu.VMEM((nf * n1, gd), BF16), pltpu.VMEM((2 * n1, nf * gd), BF16)],
        compiler_params=_params("parallel", "parallel"),
        name="dft_stage1",
    )(x3, csc, m1, twc, tws)

    y = pl.pallas_call(
        functools.partial(_dft_stage2_kernel, n2=n2, gd=gd),
        grid=(nseq, n1 // nf),
        in_specs=[
            pl.BlockSpec((nf, n2, w_f), lambda b, q: (b * (n1 // nf) + q, 0, 0)),
            pl.BlockSpec((nf, n2, w_f), lambda b, q: (b * (n1 // nf) + q, 0, 0)),
            _const_spec((n2, 2 * n2)),
        ],
        out_specs=pl.BlockSpec((n2, nf, w_f), lambda b, q: (b, q, 0)),
        out_shape=jax.ShapeDtypeStruct((nseq * n2, n1, w_f), F32),
        scratch_shapes=[pltpu.VMEM((n2 * nf, gd), F32), pltpu.VMEM((2 * n2, nf * gd), BF16)],
        compiler_params=_params("parallel", "parallel"),
        name="dft_stage2",
    )(tr, ti, c2s2)
    return y.reshape(nseq * s, w_f)


def _mixer_out_kernel(edge_ref, yfa_ref, yfb_ref, zr_ref, pc_ref, pv_ref, nc_ref, nv_ref, *rest,
                      n_a, w_c, d, x_starts):
    x_refs = rest[:len(x_starts)]
    cw_ref, cb_ref, wf_ref, wc_ref, wo_ref, o_ref = rest[len(x_starts):]
    i = pl.program_id(0)
    tm = o_ref.shape[0]
    halo = pc_ref.shape[0]
    x = x_refs[0][...]
    for x_ref, first in zip(x_refs[1:], x_starts[1:]):
        x = jnp.where(i >= first, x_ref[...], x)
    zb = zr_ref[:, 0:w_c].astype(F32)
    u = zr_ref[:, w_c:2 * w_c].astype(F32) * zr_ref[:, 2 * w_c:3 * w_c].astype(F32)
    u_prev = pc_ref[halo - 1:halo, :].astype(F32) * pv_ref[halo - 1:halo, :].astype(F32)
    u_next = nc_ref[0:1, :].astype(F32) * nv_ref[0:1, :].astype(F32)
    u_prev = jnp.where(edge_ref[i, 0] > 0, u_prev, 0.0)
    u_next = jnp.where(edge_ref[i, 1] > 0, u_next, 0.0)
    row = lax.broadcasted_iota(jnp.int32, (tm, w_c), 0)
    u_m1 = jnp.where(row == 0, u_prev, pltpu.roll(u, 1, 0))
    u_p1 = jnp.where(row == tm - 1, u_next, pltpu.roll(u, tm - 1, 0))
    cv = u_m1 * cw_ref[0:1, :] + u * cw_ref[1:2, :] + u_p1 * cw_ref[2:3, :] + cb_ref[...]
    yc = (zb * cv).astype(BF16)
    yf = jnp.where(i < n_a, yfa_ref[...], yfb_ref[...]).astype(BF16)
    gf = jax.nn.sigmoid(zr_ref[:, 3 * w_c:3 * w_c + d].astype(F32))
    gc = jax.nn.sigmoid(zr_ref[:, 3 * w_c + d:3 * w_c + 2 * d].astype(F32))
    m = (gf * jnp.dot(yf, wf_ref[...], preferred_element_type=F32)
         + gc * jnp.dot(yc, wc_ref[...], preferred_element_type=F32))
    o_ref[...] = x + jnp.dot(m.astype(BF16), wo_ref[...], preferred_element_type=F32)


def _mixer_out(seq_bounds, yf_a, yf_b, zr, x_parts, conv_w, conv_b, w_fourier, w_conv_out, w_o):
    d = x_parts[0].shape[1]
    t = sum(p.shape[0] for p in x_parts)
    w_f = w_fourier.shape[0]
    w_c = w_conv_out.shape[0]
    tm = MIXER_TILE
    halo = 2 * V7X_SUBLANES
    n_a = yf_a.shape[0] // tm
    n_b = yf_b.shape[0] // tm
    nh = t // halo
    hb = tm // halo
    assert w_c % V7X_LANES == 0 and t % tm == 0 and all(b % tm == 0 for b in seq_bounds)
    edges = jnp.asarray([[0 if i * tm in seq_bounds else 1, 0 if (i + 1) * tm in seq_bounds else 1]
                         for i in range(t // tm)], dtype=jnp.int32)
    assert all(p.shape[0] % tm == 0 for p in x_parts)
    x_tiles = [p.shape[0] // tm for p in x_parts]
    x_starts = tuple(sum(x_tiles[:k]) for k in range(len(x_parts)))
    x_specs = [pl.BlockSpec((tm, d), lambda i, e, first=first, n=n: (jnp.clip(i - first, 0, n - 1), 0))
               for first, n in zip(x_starts, x_tiles)]
    return pl.pallas_call(
        functools.partial(_mixer_out_kernel, n_a=n_a, w_c=w_c, d=d, x_starts=x_starts),
        grid_spec=pltpu.PrefetchScalarGridSpec(
            num_scalar_prefetch=1,
            grid=(t // tm,),
            in_specs=[
                pl.BlockSpec((tm, w_f), lambda i, e: (jnp.minimum(i, n_a - 1), 0)),
                pl.BlockSpec((tm, w_f), lambda i, e: (jnp.clip(i - n_a, 0, n_b - 1), 0)),
                pl.BlockSpec((tm, zr.shape[1]), lambda i, e: (i, 0)),
                pl.BlockSpec((halo, w_c), lambda i, e: (jnp.maximum(i * hb - 1, 0), 1)),
                pl.BlockSpec((halo, w_c), lambda i, e: (jnp.maximum(i * hb - 1, 0), 2)),
                pl.BlockSpec((halo, w_c), lambda i, e: (jnp.minimum((i + 1) * hb, nh - 1), 1)),
                pl.BlockSpec((halo, w_c), lambda i, e: (jnp.minimum((i + 1) * hb, nh - 1), 2)),
                *x_specs,
                _const_spec((conv_w.shape[0], w_c)),
                _const_spec((1, w_c)),
                _const_spec((w_f, d)),
                _const_spec((w_c, d)),
                _const_spec((d, d)),
            ],
            out_specs=pl.BlockSpec((tm, d), lambda i, e: (i, 0)),
        ),
        out_shape=jax.ShapeDtypeStruct((t, d), F32),
        compiler_params=_params("parallel"),
        name="mixer_out",
    )(edges, yf_a, yf_b, zr, zr, zr, zr, zr, *x_parts, conv_w, conv_b.reshape(1, w_c),
      w_fourier, w_conv_out, w_o)


def _dense_ffn_kernel(*refs, final_norm, n_cast):
    x_ref, g_ref, wg_ref, wu_ref, wd_ref, fg_ref = refs[:6]
    cast_in = refs[6:6 + n_cast]
    o_ref = refs[6 + n_cast]
    cast_out = refs[7 + n_cast:7 + 2 * n_cast]
    h_ref = refs[7 + 2 * n_cast]
    j = pl.program_id(1)

    @pl.when(j == 0)
    def _():
        h_ref[...] = _rmsnorm(x_ref[...], g_ref[...]).astype(BF16)
        o_ref[...] = x_ref[...]

    h = h_ref[...]
    a = jnp.dot(h, wg_ref[...], preferred_element_type=F32)
    b = jnp.dot(h, wu_ref[...], preferred_element_type=F32)
    act = (a * jax.nn.sigmoid(a) * b).astype(BF16)
    o_ref[...] += jnp.dot(act, wd_ref[...], preferred_element_type=F32)

    if final_norm:
        @pl.when(j == pl.num_programs(1) - 1)
        def _():
            o_ref[...] = _rmsnorm(o_ref[...], fg_ref[...])

    for src, dst in zip(cast_in, cast_out):
        dst[...] = src[...].astype(BF16)


def _cast_chunk_rows(rows, n_steps):
    step = 2 * V7X_SUBLANES
    chunk = step
    while rows % chunk or rows // chunk > n_steps:
        chunk += step
        assert chunk <= rows
    return chunk


def _dense_ffn(x, g, wg, wu, wd, final_g, cast_jobs=()):
    t, d = x.shape
    f = wg.shape[1]
    tm, tf = TOKEN_TILE, FFN_COLS
    assert t % tm == 0 and f % tf == 0
    nj = f // tf
    n_steps = (t // tm) * nj
    final_norm = final_g is not None
    fg = (final_g if final_norm else jnp.ones((d,), F32)).reshape(1, d)

    def cast_specs():
        specs = []
        for w in cast_jobs:
            rows, cols = w.shape
            chunk = _cast_chunk_rows(rows, n_steps)
            last = rows // chunk - 1
            specs.append(pl.BlockSpec(
                (chunk, cols), lambda i, j, last=last: (jnp.minimum(i * nj + j, last), 0)))
        return specs

    outs = pl.pallas_call(
        functools.partial(_dense_ffn_kernel, final_norm=final_norm, n_cast=len(cast_jobs)),
        grid=(t // tm, nj),
        in_specs=[
            pl.BlockSpec((tm, d), lambda i, j: (i, 0)),
            pl.BlockSpec((1, d), lambda i, j: (0, 0)),
            pl.BlockSpec((d, tf), lambda i, j: (0, j)),
            pl.BlockSpec((d, tf), lambda i, j: (0, j)),
            pl.BlockSpec((tf, d), lambda i, j: (j, 0)),
            pl.BlockSpec((1, d), lambda i, j: (0, 0)),
        ] + cast_specs(),
        out_specs=[pl.BlockSpec((tm, d), lambda i, j: (i, 0))] + cast_specs(),
        out_shape=[jax.ShapeDtypeStruct((t, d), F32)]
        + [jax.ShapeDtypeStruct(w.shape, BF16) for w in cast_jobs],
        scratch_shapes=[pltpu.VMEM((tm, d), BF16)],
        compiler_params=_params("arbitrary", "arbitrary"),
        name="dense_ffn",
    )(x, g.reshape(1, d), wg, wu, wd, fg, *cast_jobs)
    return outs[0], list(outs[1:])


def _router_kernel(x_ref, g_ref, rwh_ref, rwl_ref, tri_ref, ri_ref, rp_ref, cnt_ref, carry_ref, *, n_exp):
    @pl.when(pl.program_id(0) == 0)
    def _():
        carry_ref[...] = jnp.zeros_like(carry_ref)

    tm = x_ref.shape[0]
    lanes = rwh_ref.shape[1]
    h = _rmsnorm(x_ref[...], g_ref[...])
    h_hi = h.astype(BF16)
    h_lo = (h - h_hi.astype(F32)).astype(BF16)
    logits = (jnp.dot(h_hi, rwh_ref[...], preferred_element_type=F32)
              + jnp.dot(h_lo, rwh_ref[...], preferred_element_type=F32)
              + jnp.dot(h_hi, rwl_ref[...], preferred_element_type=F32))
    lane = lax.broadcasted_iota(jnp.int32, (tm, lanes), 1).astype(F32)
    neg = jnp.float32(-jnp.inf)
    lg = jnp.where(lane < n_exp, logits, neg)
    m1 = jnp.max(lg, axis=1, keepdims=True)
    i1 = jnp.min(jnp.where(lg == m1, lane, float(lanes)), axis=1, keepdims=True)
    lg2 = jnp.where(lane == i1, neg, lg)
    m2 = jnp.max(lg2, axis=1, keepdims=True)
    i2 = jnp.min(jnp.where(lg2 == m2, lane, float(lanes)), axis=1, keepdims=True)
    sel = jnp.where((lane == i1) | (lane == i2), 1.0, 0.0)
    before = jnp.dot(tri_ref[...], sel.astype(BF16), preferred_element_type=F32)
    rank = carry_ref[...] + before
    r1 = jnp.sum(jnp.where(lane == i1, rank, 0.0), axis=1, keepdims=True)
    r2 = jnp.sum(jnp.where(lane == i2, rank, 0.0), axis=1, keepdims=True)
    carry_ref[...] += jnp.sum(sel, axis=0, keepdims=True)
    cnt_ref[...] = carry_ref[...]
    e21 = jnp.exp(m2 - m1)
    p1 = 1.0 / (1.0 + e21)
    p2 = e21 / (1.0 + e21)
    rp_ref[...] = jnp.where(lane == 0, p1, jnp.where(lane == 1, p2, 0.0))
    q = jnp.where(lane == 0, i1, jnp.where(lane == 1, i2, jnp.where(lane == 2, r1, jnp.where(lane == 3, r2, 0.0))))
    ri_ref[...] = q.T[0:ri_ref.shape[0], :].astype(jnp.int32)


def _router(x, g, router_w):
    t, d = x.shape
    n_exp = router_w.shape[1]
    tm = TOKEN_TILE
    lanes = V7X_LANES
    rw = jnp.zeros((d, lanes), F32).at[:, :n_exp].set(router_w)
    rw_hi = rw.astype(BF16)
    rw_lo = (rw - rw_hi.astype(F32)).astype(BF16)
    tri =(lax.broadcasted_iota(jnp.int32, (tm, tm), 1)
           < lax.broadcasted_iota(jnp.int32, (tm, tm), 0)).astype(BF16)
    return pl.pallas_call(
        functools.partial(_router_kernel, n_exp=n_exp),
        grid=(t // tm,),
        in_specs=[
            pl.BlockSpec((tm, d), lambda i: (i, 0)),
            _const_spec((1, d)),
            _const_spec((d, lanes)),
            _const_spec((d, lanes)),
            _const_spec((tm, tm)),
        ],
        out_specs=[
            pl.BlockSpec((V7X_SUBLANES, tm), lambda i: (0, i)),
            pl.BlockSpec((tm, lanes), lambda i: (i, 0)),
            pl.BlockSpec((1, lanes), lambda i: (0, 0)),
        ],
        out_shape=[
            jax.ShapeDtypeStruct((V7X_SUBLANES, t), jnp.int32),
            jax.ShapeDtypeStruct((t, lanes), F32),
            jax.ShapeDtypeStruct((1, lanes), F32),
        ],
        scratch_shapes=[pltpu.VMEM((1, lanes), F32)],
        compiler_params=_params("arbitrary"),
        name="router",
    )(x, g.reshape(1, d), rw_hi, rw_lo, tri)


def _row_copy(src_ref, src_row, dst_ref, dst_row, sem):
    return pltpu.make_async_copy(src_ref.at[pl.ds(src_row, 1), :], dst_ref.at[pl.ds(dst_row, 1), :], sem)


def _dispatch_kernel(tend_ref, nv_ref, x_ref, g_ref, slot_ref, hs_ref, h_ref, sem, zsem, *, n_tiles):
    tm = x_ref.shape[0]

    @pl.when(pl.program_id(0) == 0)
    def _():
        h_ref[...] = jnp.zeros_like(h_ref)

        def zero_tile(tile):
            cp = pltpu.make_async_copy(h_ref, hs_ref.at[pl.ds(tile * tm, tm), :], zsem)
            cp.start()
            cp.wait()

        for e in range(tend_ref.shape[0]):
            first = tend_ref[e - 1] if e else 0

            @pl.when(tend_ref[e] > first)
            def _():
                zero_tile(tend_ref[e] - 1)

        def tail(tile, c):
            zero_tile(tile)
            return c

        lax.fori_loop(nv_ref[0], n_tiles, tail, 0)

    h_ref[...] = _rmsnorm(x_ref[...], g_ref[...])

    def issue(s, c):
        for k in range(TOP_K):
            _row_copy(h_ref, s, hs_ref, slot_ref[k * tm + s], sem).start()
        return c

    lax.fori_loop(0, tm, issue, 0, unroll=DMA_ISSUE_UNROLL)
    for _ in range(TOP_K):
        pltpu.make_async_copy(h_ref, hs_ref.at[pl.ds(0, tm), :], sem).wait()


def _dispatch(x, g, slots, tile_end, n_valid, n_tiles):
    t, d = x.shape
    tm = TOKEN_TILE
    assert tm == EXPERT_TILE
    return pl.pallas_call(
        functools.partial(_dispatch_kernel, n_tiles=n_tiles),
        grid_spec=pltpu.PrefetchScalarGridSpec(
            num_scalar_prefetch=2,
            grid=(t // tm,),
            in_specs=[
                pl.BlockSpec((tm, d), lambda i, *_: (i, 0)),
                _const_spec((1, d)),
                pl.BlockSpec((TOP_K * tm,), lambda i, *_: (i,), memory_space=pltpu.SMEM),
            ],
            out_specs=pl.BlockSpec(memory_space=pl.ANY),
            scratch_shapes=[pltpu.VMEM((tm, d), F32), pltpu.SemaphoreType.DMA(()),
                            pltpu.SemaphoreType.DMA(())],
        ),
        out_shape=jax.ShapeDtypeStruct((n_tiles * EXPERT_TILE, d), F32),
        compiler_params=_params("arbitrary"),
        name="dispatch",
    )(tile_end, n_valid, x, g.reshape(1, d), slots)


def _expert_ffn_kernel(te_ref, nv_ref, hs_ref, wg_ref, wu_ref, wd_ref, o_ref, xb_ref):
    del te_ref
    j = pl.program_id(1)
    used = pl.program_id(0) < nv_ref[0]

    @pl.when(jnp.logical_and(jnp.logical_not(used), j == 0))
    def _():
        o_ref[...] = jnp.zeros_like(o_ref)

    @pl.when(used)
    def _():
        @pl.when(j == 0)
        def _():
            xb_ref[...] = hs_ref[...].astype(BF16)
            o_ref[...] = jnp.zeros_like(o_ref)

        xb = xb_ref[...]
        a = jnp.dot(xb, wg_ref[...], preferred_element_type=F32)
        b = jnp.dot(xb, wu_ref[...], preferred_element_type=F32)
        act = (a * jax.nn.sigmoid(a) * b).astype(BF16)
        o_ref[...] += jnp.dot(act, wd_ref[...], preferred_element_type=F32)


def _expert_ffn(hs, tile_expert, n_valid, wg, wu, wd):
    n_rows, d = hs.shape
    f = wg.shape[2]
    tm, tf = EXPERT_TILE, EXPERT_COLS
    assert n_rows % tm == 0 and f % tf == 0
    nj = f // tf

    def col(g, j, nv):
        return jnp.where(g < nv[0], j, nj - 1)

    return pl.pallas_call(
        _expert_ffn_kernel,
        grid_spec=pltpu.PrefetchScalarGridSpec(
            num_scalar_prefetch=2,
            grid=(n_rows // tm, nj),
            in_specs=[
                pl.BlockSpec((tm, d), lambda g, j, te, nv: (g, 0)),
                pl.BlockSpec((None, d, tf), lambda g, j, te, nv: (te[g], 0, col(g, j, nv))),
                pl.BlockSpec((None, d, tf), lambda g, j, te, nv: (te[g], 0, col(g, j, nv))),
                pl.BlockSpec((None, tf, d), lambda g, j, te, nv: (te[g], col(g, j, nv), 0)),
            ],
            out_specs=pl.BlockSpec((tm, d), lambda g, j, te, nv: (g, 0)),
            scratch_shapes=[pltpu.VMEM((tm, d), BF16)],
        ),
        out_shape=jax.ShapeDtypeStruct((n_rows, d), F32),
        compiler_params=_params("arbitrary", "arbitrary"),
        name="expert_ffn",
    )(tile_expert, n_valid, hs, wg, wu, wd)


def _combine_kernel(x_ref, slot_ref, rp_ref, fg_ref, ys_ref, o_ref, y_ref, sem, *, final_norm):
    tm = x_ref.shape[0]

    def issue(s, c):
        for k in range(TOP_K):
            _row_copy(ys_ref, slot_ref[k * tm + s], y_ref.at[k], s, sem).start()
        return c

    lax.fori_loop(0, tm, issue, 0, unroll=DMA_ISSUE_UNROLL)
    for k in range(TOP_K):
        pltpu.make_async_copy(ys_ref.at[pl.ds(0, tm), :], y_ref.at[k], sem).wait()

    y = x_ref[...] + rp_ref[:, 0:1] * y_ref[0] + rp_ref[:, 1:2] * y_ref[1]
    if final_norm:
        y = _rmsnorm(y, fg_ref[...])
    o_ref[...] = y


def _combine(x, slots, route_p, ys, final_g, row0, rows):
    t, d = x.shape
    tm = TOKEN_TILE
    assert row0 % tm == 0 and rows % tm == 0
    b0 = row0 // tm
    final_norm = final_g is not None
    fg = (final_g if final_norm else jnp.ones((d,), F32)).reshape(1, d)
    return pl.pallas_call(
        functools.partial(_combine_kernel, final_norm=final_norm),
        grid=(rows // tm,),
        in_specs=[
            pl.BlockSpec((tm, d), lambda i: (b0 + i, 0)),
            pl.BlockSpec((TOP_K * tm,), lambda i: (b0 + i,), memory_space=pltpu.SMEM),
            pl.BlockSpec((tm, route_p.shape[1]), lambda i: (b0 + i, 0)),
            _const_spec((1, d)),
            pl.BlockSpec(memory_space=pl.ANY),
        ],
        out_specs=pl.BlockSpec((tm, d), lambda i: (i, 0)),
        scratch_shapes=[pltpu.VMEM((TOP_K, tm, d), F32), pltpu.SemaphoreType.DMA(())],
        out_shape=jax.ShapeDtypeStruct((rows, d), F32),
        compiler_params=_params("arbitrary"),
        name="combine",
    )(x, slots, route_p, fg, ys)


def _moe_plan(route_i, counts, n_exp, n_tiles):
    tm = EXPERT_TILE
    cnt = counts[0, :n_exp].astype(jnp.int32)
    tiles = (cnt + tm - 1) // tm
    tile_end = jnp.cumsum(tiles)
    offsets = (tile_end - tiles) * tm
    n_valid = tile_end[-1]
    g = jnp.arange(n_tiles, dtype=jnp.int32)
    gc = jnp.minimum(g, n_valid - 1)
    tile_expert = jnp.sum((gc[:, None] >= tile_end[None, :]).astype(jnp.int32), axis=1)
    expert_ids = jnp.arange(n_exp, dtype=jnp.int32)[None, :, None]
    chosen = route_i[0:TOP_K, None, :] == expert_ids
    slots = jnp.sum(jnp.where(chosen, offsets[None, :, None], 0), axis=1) + route_i[TOP_K:2 * TOP_K]
    slots = slots.reshape(TOP_K, -1, TOKEN_TILE).transpose(1, 0, 2).reshape(-1)
    return slots, tile_end, tile_expert, n_valid.reshape(1)


def _moe_layer(x, g, router_w, wg, wu, wd, final_g, splits):
    t, d = x.shape
    n_exp = router_w.shape[1]
    n_tiles = (t * TOP_K) // EXPERT_TILE + n_exp
    route_i, route_p, counts = _router(x, g, router_w)
    slots, tile_end, tile_expert, n_valid = _moe_plan(route_i, counts, n_exp, n_tiles)
    hs = _dispatch(x, g, slots, tile_end, n_valid, n_tiles)
    ys = _expert_ffn(hs, tile_expert, n_valid, wg, wu, wd)
    return [_combine(x, slots, route_p, ys, final_g, r0, n) for r0, n in splits]


def kernel(x_prompt, x_sample, norm1_g, w_in, conv_w, conv_b, w_fourier, w_conv_out, w_o, norm2_g,
           dense_w_gate, dense_w_up, dense_w_down, router_w, moe_w_gate, moe_w_up, moe_w_down, final_g):
    bp, sp, d = x_prompt.shape
    bs, ss, _ = x_sample.shape
    tp, ts = bp * sp, bs * ss
    t = tp + ts
    depth = norm1_g.shape[0]
    w_f = w_fourier.shape[1]
    gd = w_f // F_GROUPS
    tm = TOKEN_TILE
    assert sp % tm == 0 and ss % tm == 0

    x_parts = [x_prompt.reshape(tp, d), x_sample.reshape(ts, d)]

    seq_bounds = frozenset([b * sp for b in range(bp + 1)] + [tp + b * ss for b in range(bs + 1)])

    cc, sc = _dft_tables(gd)
    csc = jnp.concatenate([cc, sc], axis=1).astype(BF16)

    for l in range(depth):
        if l:
            x_parts = [x]
        zf, zr = _norm_inproj(x_parts, norm1_g[l], w_in[l].astype(BF16), w_f)
        yf_p = _dft_direct(zf, bp, sp, gd, csc)
        yf_s = _dft_two_stage(zf, tp, bs, ss, gd, csc)
        x = _mixer_out(seq_bounds, yf_p, yf_s, zr, x_parts, conv_w[l], conv_b[l], w_fourier[l].astype(BF16),
                       w_conv_out[l].astype(BF16), w_o[l].astype(BF16))
        last = l == depth - 1
        fg = final_g if last else None
        j = l // 2
        if l % 2 == 0:
            jobs = [] if last else [w[j].reshape(-1, w.shape[-1])
                                    for w in (moe_w_gate, moe_w_up, moe_w_down)]
            x, moe_bf16 = _dense_ffn(x, norm2_g[l], dense_w_gate[j].astype(BF16),
                                     dense_w_up[j].astype(BF16), dense_w_down[j].astype(BF16), fg, jobs)
            if last:
                outs = [x[:tp], x[tp:]]
        else:
            splits = [(0, tp), (tp, ts)] if last else [(0, t)]
            wg, wu, wd = (wb.reshape(w.shape[1:])
                          for wb, w in zip(moe_bf16, (moe_w_gate, moe_w_up, moe_w_down)))
            outs = _moe_layer(x, norm2_g[l], router_w[j], wg, wu, wd, fg, splits)
            if not last:
                x = outs[0]
    return outs[0].reshape(bp, sp, d), outs[1].reshape(bs, ss, d)
```

```python
import functools
import math

import jax
import jax.numpy as jnp
from jax import lax
from jax.experimental import pallas as pl
from jax.experimental.pallas import tpu as pltpu

F_GROUPS = 8
TOP_K = 2
EPS = 1e-6
BF16 = jnp.bfloat16
F32 = jnp.float32

V7X_LANES = 128
V7X_SUBLANES = 8
V7X_VMEM_BYTES = 64 * 1024 * 1024
VMEM_LIMIT = V7X_VMEM_BYTES - 8 * 1024 * 1024

TOKEN_TILE = 512
MIXER_TILE = 256
INPROJ_TILE = 1024
INPROJ_COLS = 1024
FFN_COLS = 512
EXPERT_TILE = 512
EXPERT_COLS = 1024
DMA_ISSUE_UNROLL = 8
DFT_RADIX = 128


def _params(*sem):
    return pltpu.CompilerParams(dimension_semantics=sem, vmem_limit_bytes=VMEM_LIMIT)


def _const_spec(shape):
    return pl.BlockSpec(shape, lambda *_: (0,) * len(shape), pipeline_mode=pl.Buffered(1))


def _rmsnorm(x, g):
    return x * lax.rsqrt(jnp.mean(x * x, axis=-1, keepdims=True) + EPS) * g


def _norm_inproj_kernel(*refs, part_tiles):
    n_parts = len(part_tiles)
    x_parts = refs[:n_parts]
    g_ref, w_ref, zf_ref, zr_ref, h_ref, xbuf, sem = refs[n_parts:]
    i = pl.program_id(0)
    j = pl.program_id(1)
    tm = xbuf.shape[1]

    def tile_copy(part, row, slot):
        return pltpu.make_async_copy(part.at[pl.ds(row, tm), :], xbuf.at[slot], sem.at[slot])

    def start_fetch(tile, slot):
        first = 0
        for part, n_tiles in zip(x_parts, part_tiles):
            @pl.when(jnp.logical_and(tile >= first, tile < first + n_tiles))
            def _(part=part, first=first):
                tile_copy(part, (tile - first) * tm, slot).start()
            first += n_tiles

    @pl.when(j == 0)
    def _():
        slot = i % 2

        @pl.when(i == 0)
        def _():
            start_fetch(i, slot)

        tile_copy(x_parts[0], 0, slot).wait()

        @pl.when(i + 1 < pl.num_programs(0))
        def _():
            start_fetch(i + 1, 1 - slot)

        h_ref[...] = _rmsnorm(xbuf[slot], g_ref[...]).astype(BF16)
        zf_ref[...] = jnp.dot(h_ref[...], w_ref[...], preferred_element_type=F32)

    @pl.when(j > 0)
    def _():
        zr_ref[...] = jnp.dot(h_ref[...], w_ref[...], preferred_element_type=F32).astype(BF16)


def _norm_inproj(x_parts, g, w, w_f):
    d = x_parts[0].shape[1]
    t = sum(p.shape[0] for p in x_parts)
    n_in = w.shape[1]
    tm, tn = INPROJ_TILE, INPROJ_COLS
    assert w_f == tn and (n_in - w_f) % tn == 0 and all(p.shape[0] % tm == 0 for p in x_parts)
    part_tiles = tuple(p.shape[0] // tm for p in x_parts)
    return pl.pallas_call(
        functools.partial(_norm_inproj_kernel, part_tiles=part_tiles),
        grid=(t // tm, n_in // tn),
        in_specs=[pl.BlockSpec(memory_space=pl.ANY) for _ in x_parts] + [
            pl.BlockSpec((1, d), lambda i, j: (0, 0)),
            pl.BlockSpec((d, tn), lambda i, j: (0, j)),
        ],
        out_specs=[
            pl.BlockSpec((tm, tn), lambda i, j: (i, 0)),
            pl.BlockSpec((tm, tn), lambda i, j: (i, jnp.maximum(j - 1, 0))),
        ],
        out_shape=[
            jax.ShapeDtypeStruct((t, w_f), F32),
            jax.ShapeDtypeStruct((t, n_in - w_f), BF16),
        ],
        scratch_shapes=[pltpu.VMEM((tm, d), BF16), pltpu.VMEM((2, tm, d), F32),
                        pltpu.SemaphoreType.DMA((2,))],
        compiler_params=_params("arbitrary", "arbitrary"),
        name="norm_inproj",
    )(*x_parts, g.reshape(1, d), w)


def _dft_tables(n):
    j = lax.broadcasted_iota(jnp.int32, (n, n), 0)
    k = lax.broadcasted_iota(jnp.int32, (n, n), 1)
    ang = ((j * k) % n).astype(F32) * (2.0 * math.pi / n)
    return jnp.cos(ang), jnp.sin(ang)


def _dft_direct_kernel(x_ref, csc_ref, css_ref, o_ref, ab_ref, *, gd):
    s = x_ref.shape[0]

    @pl.when(pl.program_id(1) == 0)
    def _():
        for g in range(x_ref.shape[1] // gd):
            cols = slice(g * gd, (g + 1) * gd)
            ab = jnp.dot(x_ref[:, cols].astype(BF16), csc_ref[...], preferred_element_type=F32)
            ab_ref[0:s, cols] = ab[:, :gd].astype(BF16)
            ab_ref[s:2 * s, cols] = ab[:, gd:].astype(BF16)

    o_ref[...] = jnp.dot(css_ref[...], ab_ref[...], preferred_element_type=F32)


def _dft_direct(zf, nseq, s, gd, csc):
    w_f = zf.shape[1]
    tm = TOKEN_TILE
    cs, sn = _dft_tables(s)
    scale = 1.0 / math.sqrt(s * gd)
    css = (jnp.concatenate([cs, -sn], axis=1) * scale).astype(BF16)
    return pl.pallas_call(
        functools.partial(_dft_direct_kernel, gd=gd),
        grid=(nseq, s // tm),
        in_specs=[
            pl.BlockSpec((s, w_f), lambda b, i: (b, 0)),
            _const_spec((gd, 2 * gd)),
            pl.BlockSpec((tm, 2 * s), lambda b, i: (i, 0)),
        ],
        out_specs=pl.BlockSpec((tm, w_f), lambda b, i: (b * (s // tm) + i, 0)),
        out_shape=jax.ShapeDtypeStruct((nseq * s, w_f), F32),
        scratch_shapes=[pltpu.VMEM((2 * s, w_f), BF16)],
        compiler_params=_params("parallel", "arbitrary"),
        name="dft_direct",
    )(zf, csc, css)


def _dft_stage1_kernel(x_ref, csc_ref, m1_ref, twc_ref, tws_ref, tr_ref, ti_ref,
                       x2, tr2, ti2, xs_ref, ab_ref, *, n1, gd):
    nf = x_ref.shape[1]
    for g in range(x_ref.shape[2] // gd):
        cols = slice(g * gd, (g + 1) * gd)
        x2[...] = x_ref[:, :, cols].reshape(n1 * nf, gd)
        for f in range(nf):
            xs_ref[f * n1:(f + 1) * n1, :] = x2[pl.ds(f, n1, stride=nf), :].astype(BF16)
        ab = jnp.dot(xs_ref[...], csc_ref[...], preferred_element_type=F32)
        for f in range(nf):
            ab_ref[0:n1, f * gd:(f + 1) * gd] = ab[f * n1:(f + 1) * n1, :gd].astype(BF16)
            ab_ref[n1:2 * n1, f * gd:(f + 1) * gd] = ab[f * n1:(f + 1) * n1, gd:].astype(BF16)
        gq = jnp.dot(m1_ref[...], ab_ref[...], preferred_element_type=F32)
        for f in range(nf):
            gr = gq[0:n1, f * gd:(f + 1) * gd]
            gi = gq[n1:2 * n1, f * gd:(f + 1) * gd]
            tc = twc_ref[f]
            ts = tws_ref[f]
            tr2[pl.ds(f, n1, stride=nf), :] = gr * tc + gi * ts
            ti2[pl.ds(f, n1, stride=nf), :] = gi * tc - gr * ts
        tr_ref[:, :, cols] = tr2[...].reshape(n1, nf, gd)
        ti_ref[:, :, cols] = ti2[...].reshape(n1, nf, gd)


def _dft_stage2_kernel(tr_ref, ti_ref, c2s2_ref, o_ref, o2, tcat_ref, *, n2, gd):
    nk = tr_ref.shape[0]
    for g in range(tr_ref.shape[2] // gd):
        cols = slice(g * gd, (g + 1) * gd)
        for q in range(nk):
            tcat_ref[0:n2, q * gd:(q + 1) * gd] = tr_ref[q, :, cols].astype(BF16)
            tcat_ref[n2:2 * n2, q * gd:(q + 1) * gd] = ti_ref[q, :, cols].astype(BF16)
        y = jnp.dot(c2s2_ref[...], tcat_ref[...], preferred_element_type=F32)
        for q in range(nk):
            o2[pl.ds(q, n2, stride=nk), :] = y[:, q * gd:(q + 1) * gd]
        o_ref[:, :, cols] = o2[...].reshape(n2, nk, gd)


def _dft_two_stage(zf, row0, nseq, s, gd, csc):
    w_f = zf.shape[1]
    n1 = DFT_RADIX
    n2 = s // n1
    nf = V7X_SUBLANES
    assert s % n1 == 0 and n2 % nf == 0 and n1 % nf == 0 and gd == V7X_LANES
    if row0 % s == 0:
        blk0 = row0 // s
    else:
        zf, blk0 = zf[row0:], 0
    scale = 1.0 / math.sqrt(s * gd)

    c1, s1 = _dft_tables(n1)
    m1 = jnp.concatenate([jnp.concatenate([c1, -s1], axis=1),
                          jnp.concatenate([-s1, -c1], axis=1)], axis=0).astype(BF16)
    a = lax.broadcasted_iota(jnp.int32, (n2, n1), 0)
    b = lax.broadcasted_iota(jnp.int32, (n2, n1), 1)
    ang = ((a * b) % s).astype(F32) * (2.0 * math.pi / s)
    twc = jnp.broadcast_to((jnp.cos(ang) * scale)[:, :, None], (n2, n1, gd))
    tws = jnp.broadcast_to((jnp.sin(ang) * scale)[:, :, None], (n2, n1, gd))
    c2, s2 = _dft_tables(n2)
    c2s2 = jnp.concatenate([c2, s2], axis=1).astype(BF16)

    x3 = zf.reshape(zf.shape[0] // n2, n2, w_f)
    t_shape = jax.ShapeDtypeStruct((nseq * n1, n2, w_f), F32)
    tr, ti = pl.pallas_call(
        functools.partial(_dft_stage1_kernel, n1=n1, gd=gd),
        grid=(nseq, n2 // nf),
        in_specs=[
            pl.BlockSpec((n1, nf, w_f), lambda b, f: (blk0 + b, f, 0)),
            _const_spec((gd, 2 * gd)),
            _const_spec((2 * n1, 2 * n1)),
            pl.BlockSpec((nf, n1, gd), lambda b, f: (f, 0, 0)),
            pl.BlockSpec((nf, n1, gd), lambda b, f: (f, 0, 0)),
        ],
        out_specs=[
            pl.BlockSpec((n1, nf, w_f), lambda b, f: (b, f, 0)),
            pl.BlockSpec((n1, nf, w_f), lambda b, f: (b, f, 0)),
        ],
        out_shape=[t_shape, t_shape],
        scratch_shapes=[pltpu.VMEM((n1 * nf, gd), F32), pltpu.VMEM((n1 * nf, gd), F32),
                        pltpu.VMEM((n1 * nf, gd), F32),
                        pltpu.VMEM((nf * n1, gd), BF16), pltpu.VMEM((2 * n1, nf * gd), BF16)],
        compiler_params=_params("parallel", "parallel"),
        name="dft_stage1",
    )(x3, csc, m1, twc, tws)

    y = pl.pallas_call(
        functools.partial(_dft_stage2_kernel, n2=n2, gd=gd),
        grid=(nseq, n1 // nf),
        in_specs=[
            pl.BlockSpec((nf, n2, w_f), lambda b, q: (b * (n1 // nf) + q, 0, 0)),
            pl.BlockSpec((nf, n2, w_f), lambda b, q: (b * (n1 // nf) + q, 0, 0)),
            _const_spec((n2, 2 * n2)),
        ],
        out_specs=pl.BlockSpec((n2, nf, w_f), lambda b, q: (b, q, 0)),
        out_shape=jax.ShapeDtypeStruct((nseq * n2, n1, w_f), F32),
        scratch_shapes=[pltpu.VMEM((n2 * nf, gd), F32), pltpu.VMEM((2 * n2, nf * gd), BF16)],
        compiler_params=_params("parallel", "parallel"),
        name="dft_stage2",
    )(tr, ti, c2s2)
    return y.reshape(nseq * s, w_f)


def _mixer_out_kernel(edge_ref, yfa_ref, yfb_ref, zr_ref, pc_ref, pv_ref, nc_ref, nv_ref, *rest,
                      n_a, w_c, d, x_starts):
    x_refs = rest[:len(x_starts)]
    cw_ref, cb_ref, wf_ref, wc_ref, wo_ref, o_ref = rest[len(x_starts):]
    i = pl.program_id(0)
    tm = o_ref.shape[0]
    halo = pc_ref.shape[0]
    x = x_refs[0][...]
    for x_ref, first in zip(x_refs[1:], x_starts[1:]):
        x = jnp.where(i >= first, x_ref[...], x)
    zb = zr_ref[:, 0:w_c].astype(F32)
    u = zr_ref[:, w_c:2 * w_c].astype(F32) * zr_ref[:, 2 * w_c:3 * w_c].astype(F32)
    u_prev = pc_ref[halo - 1:halo, :].astype(F32) * pv_ref[halo - 1:halo, :].astype(F32)
    u_next = nc_ref[0:1, :].astype(F32) * nv_ref[0:1, :].astype(F32)
    u_prev = jnp.where(edge_ref[i, 0] > 0, u_prev, 0.0)
    u_next = jnp.where(edge_ref[i, 1] > 0, u_next, 0.0)
    row = lax.broadcasted_iota(jnp.int32, (tm, w_c), 0)
    u_m1 = jnp.where(row == 0, u_prev, pltpu.roll(u, 1, 0))
    u_p1 = jnp.where(row == tm - 1, u_next, pltpu.roll(u, tm - 1, 0))
    cv = u_m1 * cw_ref[0:1, :] + u * cw_ref[1:2, :] + u_p1 * cw_ref[2:3, :] + cb_ref[...]
    yc = (zb * cv).astype(BF16)
    yf = jnp.where(i < n_a, yfa_ref[...], yfb_ref[...]).astype(BF16)
    gf = jax.nn.sigmoid(zr_ref[:, 3 * w_c:3 * w_c + d].astype(F32))
    gc = jax.nn.sigmoid(zr_ref[:, 3 * w_c + d:3 * w_c + 2 * d].astype(F32))
    m = (gf * jnp.dot(yf, wf_ref[...], preferred_element_type=F32)
         + gc * jnp.dot(yc, wc_ref[...], preferred_element_type=F32))
    o_ref[...] = x + jnp.dot(m.astype(BF16), wo_ref[...], preferred_element_type=F32)


def _mixer_out(seq_bounds, yf_a, yf_b, zr, x_parts, conv_w, conv_b, w_fourier, w_conv_out, w_o):
    d = x_parts[0].shape[1]
    t = sum(p.shape[0] for p in x_parts)
    w_f = w_fourier.shape[0]
    w_c = w_conv_out.shape[0]
    tm = MIXER_TILE
    halo = 2 * V7X_SUBLANES
    n_a = yf_a.shape[0] // tm
    n_b = yf_b.shape[0] // tm
    nh = t // halo
    hb = tm // halo
    assert w_c % V7X_LANES == 0 and t % tm == 0 and all(b % tm == 0 for b in seq_bounds)
    edges = jnp.asarray([[0 if i * tm in seq_bounds else 1, 0 if (i + 1) * tm in seq_bounds else 1]
                         for i in range(t // tm)], dtype=jnp.int32)
    assert all(p.shape[0] % tm == 0 for p in x_parts)
    x_tiles = [p.shape[0] // tm for p in x_parts]
    x_starts = tuple(sum(x_tiles[:k]) for k in range(len(x_parts)))
    x_specs = [pl.BlockSpec((tm, d), lambda i, e, first=first, n=n: (jnp.clip(i - first, 0, n - 1), 0))
               for first, n in zip(x_starts, x_tiles)]
    return pl.pallas_call(
        functools.partial(_mixer_out_kernel, n_a=n_a, w_c=w_c, d=d, x_starts=x_starts),
        grid_spec=pltpu.PrefetchScalarGridSpec(
            num_scalar_prefetch=1,
            grid=(t // tm,),
            in_specs=[
                pl.BlockSpec((tm, w_f), lambda i, e: (jnp.minimum(i, n_a - 1), 0)),
                pl.BlockSpec((tm, w_f), lambda i, e: (jnp.clip(i - n_a, 0, n_b - 1), 0)),
                pl.BlockSpec((tm, zr.shape[1]), lambda i, e: (i, 0)),
                pl.BlockSpec((halo, w_c), lambda i, e: (jnp.maximum(i * hb - 1, 0), 1)),
                pl.BlockSpec((halo, w_c), lambda i, e: (jnp.maximum(i * hb - 1, 0), 2)),
                pl.BlockSpec((halo, w_c), lambda i, e: (jnp.minimum((i + 1) * hb, nh - 1), 1)),
                pl.BlockSpec((halo, w_c), lambda i, e: (jnp.minimum((i + 1) * hb, nh - 1), 2)),
                *x_specs,
                _const_spec((conv_w.shape[0], w_c)),
                _const_spec((1, w_c)),
                _const_spec((w_f, d)),
                _const_spec((w_c, d)),
                _const_spec((d, d)),
            ],
            out_specs=pl.BlockSpec((tm, d), lambda i, e: (i, 0)),
        ),
        out_shape=jax.ShapeDtypeStruct((t, d), F32),
        compiler_params=_params("parallel"),
        name="mixer_out",
    )(edges, yf_a, yf_b, zr, zr, zr, zr, zr, *x_parts, conv_w, conv_b.reshape(1, w_c),
      w_fourier, w_conv_out, w_o)


def _dense_ffn_kernel(*refs, final_norm, n_cast):
    x_ref, g_ref, wg_ref, wu_ref, wd_ref, fg_ref = refs[:6]
    cast_in = refs[6:6 + n_cast]
    o_ref = refs[6 + n_cast]
    cast_out = refs[7 + n_cast:7 + 2 * n_cast]
    h_ref = refs[7 + 2 * n_cast]
    j = pl.program_id(1)

    @pl.when(j == 0)
    def _():
        h_ref[...] = _rmsnorm(x_ref[...], g_ref[...]).astype(BF16)
        o_ref[...] = x_ref[...]

    h = h_ref[...]
    a = jnp.dot(h, wg_ref[...], preferred_element_type=F32)
    b = jnp.dot(h, wu_ref[...], preferred_element_type=F32)
    act = (a * jax.nn.sigmoid(a) * b).astype(BF16)
    o_ref[...] += jnp.dot(act, wd_ref[...], preferred_element_type=F32)

    if final_norm:
        @pl.when(j == pl.num_programs(1) - 1)
        def _():
            o_ref[...] = _rmsnorm(o_ref[...], fg_ref[...])

    for src, dst in zip(cast_in, cast_out):
        dst[...] = src[...].astype(BF16)


def _cast_chunk_rows(rows, n_steps):
    step = 2 * V7X_SUBLANES
    chunk = step
    while rows % chunk or rows // chunk > n_steps:
        chunk += step
        assert chunk <= rows
    return chunk


def _dense_ffn(x, g, wg, wu, wd, final_g, cast_jobs=()):
    t, d = x.shape
    f = wg.shape[1]
    tm, tf = TOKEN_TILE, FFN_COLS
    assert t % tm == 0 and f % tf == 0
    nj = f // tf
    n_steps = (t // tm) * nj
    final_norm = final_g is not None
    fg = (final_g if final_norm else jnp.ones((d,), F32)).reshape(1, d)

    def cast_specs():
        specs = []
        for w in cast_jobs:
            rows, cols = w.shape
            chunk = _cast_chunk_rows(rows, n_steps)
            last = rows // chunk - 1
            specs.append(pl.BlockSpec(
                (chunk, cols), lambda i, j, last=last: (jnp.minimum(i * nj + j, last), 0)))
        return specs

    outs = pl.pallas_call(
        functools.partial(_dense_ffn_kernel, final_norm=final_norm, n_cast=len(cast_jobs)),
        grid=(t // tm, nj),
        in_specs=[
            pl.BlockSpec((tm, d), lambda i, j: (i, 0)),
            pl.BlockSpec((1, d), lambda i, j: (0, 0)),
            pl.BlockSpec((d, tf), lambda i, j: (0, j)),
            pl.BlockSpec((d, tf), lambda i, j: (0, j)),
            pl.BlockSpec((tf, d), lambda i, j: (j, 0)),
            pl.BlockSpec((1, d), lambda i, j: (0, 0)),
        ] + cast_specs(),
        out_specs=[pl.BlockSpec((tm, d), lambda i, j: (i, 0))] + cast_specs(),
        out_shape=[jax.ShapeDtypeStruct((t, d), F32)]
        + [jax.ShapeDtypeStruct(w.shape, BF16) for w in cast_jobs],
        scratch_shapes=[pltpu.VMEM((tm, d), BF16)],
        compiler_params=_params("arbitrary", "arbitrary"),
        name="dense_ffn",
    )(x, g.reshape(1, d), wg, wu, wd, fg, *cast_jobs)
    return outs[0], list(outs[1:])


def _router_kernel(x_ref, g_ref, rwh_ref, rwl_ref, tri_ref, ri_ref, rp_ref, cnt_ref, carry_ref, *, n_exp):
    @pl.when(pl.program_id(0) == 0)
    def _():
        carry_ref[...] = jnp.zeros_like(carry_ref)

    tm = x_ref.shape[0]
    lanes = rwh_ref.shape[1]
    h = _rmsnorm(x_ref[...], g_ref[...])
    h_hi = h.astype(BF16)
    h_lo = (h - h_hi.astype(F32)).astype(BF16)
    logits = (jnp.dot(h_hi, rwh_ref[...], preferred_element_type=F32)
              + jnp.dot(h_lo, rwh_ref[...], preferred_element_type=F32)
              + jnp.dot(h_hi, rwl_ref[...], preferred_element_type=F32))
    lane = lax.broadcasted_iota(jnp.int32, (tm, lanes), 1).astype(F32)
    neg = jnp.float32(-jnp.inf)
    lg = jnp.where(lane < n_exp, logits, neg)
    m1 = jnp.max(lg, axis=1, keepdims=True)
    i1 = jnp.min(jnp.where(lg == m1, lane, float(lanes)), axis=1, keepdims=True)
    lg2 = jnp.where(lane == i1, neg, lg)
    m2 = jnp.max(lg2, axis=1, keepdims=True)
    i2 = jnp.min(jnp.where(lg2 == m2, lane, float(lanes)), axis=1, keepdims=True)
    sel = jnp.where((lane == i1) | (lane == i2), 1.0, 0.0)
    before = jnp.dot(tri_ref[...], sel.astype(BF16), preferred_element_type=F32)
    rank = carry_ref[...] + before
    r1 = jnp.sum(jnp.where(lane == i1, rank, 0.0), axis=1, keepdims=True)
    r2 = jnp.sum(jnp.where(lane == i2, rank, 0.0), axis=1, keepdims=True)
    carry_ref[...] += jnp.sum(sel, axis=0, keepdims=True)
    cnt_ref[...] = carry_ref[...]
    e21 = jnp.exp(m2 - m1)
    p1 = 1.0 / (1.0 + e21)
    p2 = e21 / (1.0 + e21)
    rp_ref[...] = jnp.where(lane == 0, p1, jnp.where(lane == 1, p2, 0.0))
    q = jnp.where(lane == 0, i1, jnp.where(lane == 1, i2, jnp.where(lane == 2, r1, jnp.where(lane == 3, r2, 0.0))))
    ri_ref[...] = q.T[0:ri_ref.shape[0], :].astype(jnp.int32)


def _router(x, g, router_w):
    t, d = x.shape
    n_exp = router_w.shape[1]
    tm = TOKEN_TILE
    lanes = V7X_LANES
    rw = jnp.zeros((d, lanes), F32).at[:, :n_exp].set(router_w)
    rw_hi = rw.astype(BF16)
    rw_lo = (rw - rw_hi.astype(F32)).astype(BF16)
    tri =(lax.broadcasted_iota(jnp.int32, (tm, tm), 1)
           < lax.broadcasted_iota(jnp.int32, (tm, tm), 0)).astype(BF16)
    return pl.pallas_call(
        functools.partial(_router_kernel, n_exp=n_exp),
        grid=(t // tm,),
        in_specs=[
            pl.BlockSpec((tm, d), lambda i: (i, 0)),
            _const_spec((1, d)),
            _const_spec((d, lanes)),
            _const_spec((d, lanes)),
            _const_spec((tm, tm)),
        ],
        out_specs=[
            pl.BlockSpec((V7X_SUBLANES, tm), lambda i: (0, i)),
            pl.BlockSpec((tm, lanes), lambda i: (i, 0)),
            pl.BlockSpec((1, lanes), lambda i: (0, 0)),
        ],
        out_shape=[
            jax.ShapeDtypeStruct((V7X_SUBLANES, t), jnp.int32),
            jax.ShapeDtypeStruct((t, lanes), F32),
            jax.ShapeDtypeStruct((1, lanes), F32),
        ],
        scratch_shapes=[pltpu.VMEM((1, lanes), F32)],
        compiler_params=_params("arbitrary"),
        name="router",
    )(x, g.reshape(1, d), rw_hi, rw_lo, tri)


def _row_copy(src_ref, src_row, dst_ref, dst_row, sem):
    return pltpu.make_async_copy(src_ref.at[pl.ds(src_row, 1), :], dst_ref.at[pl.ds(dst_row, 1), :], sem)


def _dispatch_kernel(tend_ref, nv_ref, x_ref, g_ref, slot_ref, hs_ref, h_ref, sem, zsem, *, n_tiles):
    tm = x_ref.shape[0]

    @pl.when(pl.program_id(0) == 0)
    def _():
        h_ref[...] = jnp.zeros_like(h_ref)

        def zero_tile(tile):
            cp = pltpu.make_async_copy(h_ref, hs_ref.at[pl.ds(tile * tm, tm), :], zsem)
            cp.start()
            cp.wait()

        for e in range(tend_ref.shape[0]):
            first = tend_ref[e - 1] if e else 0

            @pl.when(tend_ref[e] > first)
            def _():
                zero_tile(tend_ref[e] - 1)

        def tail(tile, c):
            zero_tile(tile)
            return c

        lax.fori_loop(nv_ref[0], n_tiles, tail, 0)

    h_ref[...] = _rmsnorm(x_ref[...], g_ref[...])

    def issue(s, c):
        for k in range(TOP_K):
            _row_copy(h_ref, s, hs_ref, slot_ref[k * tm + s], sem).start()
        return c

    lax.fori_loop(0, tm, issue, 0, unroll=DMA_ISSUE_UNROLL)
    for _ in range(TOP_K):
        pltpu.make_async_copy(h_ref, hs_ref.at[pl.ds(0, tm), :], sem).wait()


def _dispatch(x, g, slots, tile_end, n_valid, n_tiles):
    t, d = x.shape
    tm = TOKEN_TILE
    assert tm == EXPERT_TILE
    return pl.pallas_call(
        functools.partial(_dispatch_kernel, n_tiles=n_tiles),
        grid_spec=pltpu.PrefetchScalarGridSpec(
            num_scalar_prefetch=2,
            grid=(t // tm,),
            in_specs=[
                pl.BlockSpec((tm, d), lambda i, *_: (i, 0)),
                _const_spec((1, d)),
                pl.BlockSpec((TOP_K * tm,), lambda i, *_: (i,), memory_space=pltpu.SMEM),
            ],
            out_specs=pl.BlockSpec(memory_space=pl.ANY),
            scratch_shapes=[pltpu.VMEM((tm, d), F32), pltpu.SemaphoreType.DMA(()),
                            pltpu.SemaphoreType.DMA(())],
        ),
        out_shape=jax.ShapeDtypeStruct((n_tiles * EXPERT_TILE, d), F32),
        compiler_params=_params("arbitrary"),
        name="dispatch",
    )(tile_end, n_valid, x, g.reshape(1, d), slots)


def _expert_ffn_kernel(te_ref, nv_ref, hs_ref, wg_ref, wu_ref, wd_ref, o_ref, xb_ref):
    del te_ref
    j = pl.program_id(1)
    used = pl.program_id(0) < nv_ref[0]

    @pl.when(jnp.logical_and(jnp.logical_not(used), j == 0))
    def _():
        o_ref[...] = jnp.zeros_like(o_ref)

    @pl.when(used)
    def _():
        @pl.when(j == 0)
        def _():
            xb_ref[...] = hs_ref[...].astype(BF16)
            o_ref[...] = jnp.zeros_like(o_ref)

        xb = xb_ref[...]
        a = jnp.dot(xb, wg_ref[...], preferred_element_type=F32)
        b = jnp.dot(xb, wu_ref[...], preferred_element_type=F32)
        act = (a * jax.nn.sigmoid(a) * b).astype(BF16)
        o_ref[...] += jnp.dot(act, wd_ref[...], preferred_element_type=F32)


def _expert_ffn(hs, tile_expert, n_valid, wg, wu, wd):
    n_rows, d = hs.shape
    f = wg.shape[2]
    tm, tf = EXPERT_TILE, EXPERT_COLS
    assert n_rows % tm == 0 and f % tf == 0
    nj = f // tf

    def col(g, j, nv):
        return jnp.where(g < nv[0], j, nj - 1)

    return pl.pallas_call(
        _expert_ffn_kernel,
        grid_spec=pltpu.PrefetchScalarGridSpec(
            num_scalar_prefetch=2,
            grid=(n_rows // tm, nj),
            in_specs=[
                pl.BlockSpec((tm, d), lambda g, j, te, nv: (g, 0)),
                pl.BlockSpec((None, d, tf), lambda g, j, te, nv: (te[g], 0, col(g, j, nv))),
                pl.BlockSpec((None, d, tf), lambda g, j, te, nv: (te[g], 0, col(g, j, nv))),
                pl.BlockSpec((None, tf, d), lambda g, j, te, nv: (te[g], col(g, j, nv), 0)),
            ],
            out_specs=pl.BlockSpec((tm, d), lambda g, j, te, nv: (g, 0)),
            scratch_shapes=[pltpu.VMEM((tm, d), BF16)],
        ),
        out_shape=jax.ShapeDtypeStruct((n_rows, d), F32),
        compiler_params=_params("arbitrary", "arbitrary"),
        name="expert_ffn",
    )(tile_expert, n_valid, hs, wg, wu, wd)


def _combine_kernel(x_ref, slot_ref, next_slot_ref, rp_ref, fg_ref, ys_ref, o_ref, y_ref, sem, *,
                    final_norm):
    tm = x_ref.shape[0]
    i = pl.program_id(0)
    cur = i % 2

    def gather(slots, buf):
        def issue(s, c):
            for k in range(TOP_K):
                _row_copy(ys_ref, slots[k * tm + s], y_ref.at[buf * TOP_K + k], s, sem.at[buf]).start()
            return c

        lax.fori_loop(0, tm, issue, 0, unroll=DMA_ISSUE_UNROLL)

    @pl.when(i == 0)
    def _():
        gather(slot_ref, cur)

    @pl.when(i + 1 < pl.num_programs(0))
    def _():
        gather(next_slot_ref, 1 - cur)

    for k in range(TOP_K):
        pltpu.make_async_copy(ys_ref.at[pl.ds(0, tm), :], y_ref.at[cur * TOP_K + k], sem.at[cur]).wait()

    y = (x_ref[...] + rp_ref[:, 0:1] * y_ref[cur * TOP_K]
         + rp_ref[:, 1:2] * y_ref[cur * TOP_K + 1])
    if final_norm:
        y = _rmsnorm(y, fg_ref[...])
    o_ref[...] = y


def _combine(x, slots, route_p, ys, final_g, row0, rows):
    t, d = x.shape
    tm = TOKEN_TILE
    assert row0 % tm == 0 and rows % tm == 0
    b0 = row0 // tm
    last = b0 + rows // tm - 1
    final_norm = final_g is not None
    fg = (final_g if final_norm else jnp.ones((d,), F32)).reshape(1, d)
    return pl.pallas_call(
        functools.partial(_combine_kernel, final_norm=final_norm),
        grid=(rows // tm,),
        in_specs=[
            pl.BlockSpec((tm, d), lambda i: (b0 + i, 0)),
            pl.BlockSpec((TOP_K * tm,), lambda i: (b0 + i,), memory_space=pltpu.SMEM),
            pl.BlockSpec((TOP_K * tm,), lambda i: (jnp.minimum(b0 + i + 1, last),),
                         memory_space=pltpu.SMEM),
            pl.BlockSpec((tm, route_p.shape[1]), lambda i: (b0 + i, 0)),
            _const_spec((1, d)),
            pl.BlockSpec(memory_space=pl.ANY),
        ],
        out_specs=pl.BlockSpec((tm, d), lambda i: (i, 0)),
        scratch_shapes=[pltpu.VMEM((2 * TOP_K, tm, d), F32), pltpu.SemaphoreType.DMA((2,))],
        out_shape=jax.ShapeDtypeStruct((rows, d), F32),
        compiler_params=_params("arbitrary"),
        name="combine",
    )(x, slots, slots, route_p, fg, ys)


def _moe_plan(route_i, counts, n_exp, n_tiles):
    tm = EXPERT_TILE
    cnt = counts[0, :n_exp].astype(jnp.int32)
    tiles = (cnt + tm - 1) // tm
    tile_end = jnp.cumsum(tiles)
    offsets = (tile_end - tiles) * tm
    n_valid = tile_end[-1]
    g = jnp.arange(n_tiles, dtype=jnp.int32)
    gc = jnp.minimum(g, n_valid - 1)
    tile_expert = jnp.sum((gc[:, None] >= tile_end[None, :]).astype(jnp.int32), axis=1)
    expert_ids = jnp.arange(n_exp, dtype=jnp.int32)[None, :, None]
    chosen = route_i[0:TOP_K, None, :] == expert_ids
    slots = jnp.sum(jnp.where(chosen, offsets[None, :, None], 0), axis=1) + route_i[TOP_K:2 * TOP_K]
    slots = slots.reshape(TOP_K, -1, TOKEN_TILE).transpose(1, 0, 2).reshape(-1)
    return slots, tile_end, tile_expert, n_valid.reshape(1)


def _moe_layer(x, g, router_w, wg, wu, wd, final_g, splits):
    t, d = x.shape
    n_exp = router_w.shape[1]
    n_tiles = (t * TOP_K) // EXPERT_TILE + n_exp
    route_i, route_p, counts = _router(x, g, router_w)
    slots, tile_end, tile_expert, n_valid = _moe_plan(route_i, counts, n_exp, n_tiles)
    hs = _dispatch(x, g, slots, tile_end, n_valid, n_tiles)
    ys = _expert_ffn(hs, tile_expert, n_valid, wg, wu, wd)
    return [_combine(x, slots, route_p, ys, final_g, r0, n) for r0, n in splits]


def kernel(x_prompt, x_sample, norm1_g, w_in, conv_w, conv_b, w_fourier, w_conv_out, w_o, norm2_g,
           dense_w_gate, dense_w_up, dense_w_down, router_w, moe_w_gate, moe_w_up, moe_w_down, final_g):
    bp, sp, d = x_prompt.shape
    bs, ss, _ = x_sample.shape
    tp, ts = bp * sp, bs * ss
    t = tp + ts
    depth = norm1_g.shape[0]
    w_f = w_fourier.shape[1]
    gd = w_f // F_GROUPS
    tm = TOKEN_TILE
    assert sp % tm == 0 and ss % tm == 0

    x_parts = [x_prompt.reshape(tp, d), x_sample.reshape(ts, d)]

    seq_bounds = frozenset([b * sp for b in range(bp + 1)] + [tp + b * ss for b in range(bs + 1)])

    cc, sc = _dft_tables(gd)
    csc = jnp.concatenate([cc, sc], axis=1).astype(BF16)

    for l in range(depth):
        if l:
            x_parts = [x]
        zf, zr = _norm_inproj(x_parts, norm1_g[l], w_in[l].astype(BF16), w_f)
        yf_p = _dft_direct(zf, bp, sp, gd, csc)
        yf_s = _dft_two_stage(zf, tp, bs, ss, gd, csc)
        x = _mixer_out(seq_bounds, yf_p, yf_s, zr, x_parts, conv_w[l], conv_b[l], w_fourier[l].astype(BF16),
                       w_conv_out[l].astype(BF16), w_o[l].astype(BF16))
        last = l == depth - 1
        fg = final_g if last else None
        j = l // 2
        if l % 2 == 0:
            jobs = [] if last else [w[j].reshape(-1, w.shape[-1])
                                    for w in (moe_w_gate, moe_w_up, moe_w_down)]
            x, moe_bf16 = _dense_ffn(x, norm2_g[l], dense_w_gate[j].astype(BF16),
                                     dense_w_up[j].astype(BF16), dense_w_down[j].astype(BF16), fg, jobs)
            if last:
                outs = [x[:tp], x[tp:]]
        else:
            splits = [(0, tp), (tp, ts)] if last else [(0, t)]
            wg, wu, wd = (wb.reshape(w.shape[1:])
                          for wb, w in zip(moe_bf16, (moe_w_gate, moe_w_up, moe_w_down)))
            outs = _moe_layer(x, norm2_g[l], router_w[j], wg, wu, wd, fg, splits)
            if not last:
                x = outs[0]
    return outs[0].reshape(bp, sp, d), outs[1].reshape(bs, ss, d)
```

```python
import functools
import math

import jax
import jax.numpy as jnp
from jax import lax
from jax.experimental import pallas as pl
from jax.experimental.pallas import tpu as pltpu

F_GROUPS = 8
TOP_K = 2
EPS = 1e-6
BF16 = jnp.bfloat16
F32 = jnp.float32

V7X_LANES = 128
V7X_SUBLANES = 8
V7X_VMEM_BYTES = 64 * 1024 * 1024
VMEM_LIMIT = V7X_VMEM_BYTES - 8 * 1024 * 1024

TOKEN_TILE = 512
MIXER_TILE = 256
INPROJ_TILE = 1024
INPROJ_COLS = 1024
FFN_COLS = 512
EXPERT_TILE = 512
EXPERT_COLS = 1024
DMA_ISSUE_UNROLL = 8
DFT_RADIX = 128


def _params(*sem):
    return pltpu.CompilerParams(dimension_semantics=sem, vmem_limit_bytes=VMEM_LIMIT)


def _const_spec(shape):
    return pl.BlockSpec(shape, lambda *_: (0,) * len(shape), pipeline_mode=pl.Buffered(1))


def _rmsnorm(x, g):
    return x * lax.rsqrt(jnp.mean(x * x, axis=-1, keepdims=True) + EPS) * g


def _norm_inproj_kernel(*refs, part_tiles):
    n_parts = len(part_tiles)
    x_parts = refs[:n_parts]
    g_ref, w_ref, zf_ref, zr_ref, h_ref, xbuf, sem = refs[n_parts:]
    i = pl.program_id(0)
    j = pl.program_id(1)
    tm = xbuf.shape[1]

    def tile_copy(part, row, slot):
        return pltpu.make_async_copy(part.at[pl.ds(row, tm), :], xbuf.at[slot], sem.at[slot])

    def start_fetch(tile, slot):
        first = 0
        for part, n_tiles in zip(x_parts, part_tiles):
            @pl.when(jnp.logical_and(tile >= first, tile < first + n_tiles))
            def _(part=part, first=first):
                tile_copy(part, (tile - first) * tm, slot).start()
            first += n_tiles

    @pl.when(j == 0)
    def _():
        slot = i % 2

        @pl.when(i == 0)
        def _():
            start_fetch(i, slot)

        tile_copy(x_parts[0], 0, slot).wait()

        @pl.when(i + 1 < pl.num_programs(0))
        def _():
            start_fetch(i + 1, 1 - slot)

        h_ref[...] = _rmsnorm(xbuf[slot], g_ref[...]).astype(BF16)
        zf_ref[...] = jnp.dot(h_ref[...], w_ref[...], preferred_element_type=F32)

    @pl.when(j > 0)
    def _():
        zr_ref[...] = jnp.dot(h_ref[...], w_ref[...], preferred_element_type=F32).astype(BF16)


def _norm_inproj(x_parts, g, w, w_f):
    d = x_parts[0].shape[1]
    t = sum(p.shape[0] for p in x_parts)
    n_in = w.shape[1]
    tm, tn = INPROJ_TILE, INPROJ_COLS
    assert w_f == tn and (n_in - w_f) % tn == 0 and all(p.shape[0] % tm == 0 for p in x_parts)
    part_tiles = tuple(p.shape[0] // tm for p in x_parts)
    return pl.pallas_call(
        functools.partial(_norm_inproj_kernel, part_tiles=part_tiles),
        grid=(t // tm, n_in // tn),
        in_specs=[pl.BlockSpec(memory_space=pl.ANY) for _ in x_parts] + [
            pl.BlockSpec((1, d), lambda i, j: (0, 0)),
            pl.BlockSpec((d, tn), lambda i, j: (0, j)),
        ],
        out_specs=[
            pl.BlockSpec((tm, tn), lambda i, j: (i, 0)),
            pl.BlockSpec((tm, tn), lambda i, j: (i, jnp.maximum(j - 1, 0))),
        ],
        out_shape=[
            jax.ShapeDtypeStruct((t, w_f), F32),
            jax.ShapeDtypeStruct((t, n_in - w_f), BF16),
        ],
        scratch_shapes=[pltpu.VMEM((tm, d), BF16), pltpu.VMEM((2, tm, d), F32),
                        pltpu.SemaphoreType.DMA((2,))],
        compiler_params=_params("arbitrary", "arbitrary"),
        name="norm_inproj",
    )(*x_parts, g.reshape(1, d), w)


def _dft_tables(n):
    j = lax.broadcasted_iota(jnp.int32, (n, n), 0)
    k = lax.broadcasted_iota(jnp.int32, (n, n), 1)
    ang = ((j * k) % n).astype(F32) * (2.0 * math.pi / n)
    return jnp.cos(ang), jnp.sin(ang)


def _dft_direct_kernel(x_ref, csc_ref, css_ref, o_ref, ab_ref, *, gd):
    s = x_ref.shape[0]

    @pl.when(pl.program_id(1) == 0)
    def _():
        for g in range(x_ref.shape[1] // gd):
            cols = slice(g * gd, (g + 1) * gd)
            ab = jnp.dot(x_ref[:, cols].astype(BF16), csc_ref[...], preferred_element_type=F32)
            ab_ref[0:s, cols] = ab[:, :gd].astype(BF16)
            ab_ref[s:2 * s, cols] = ab[:, gd:].astype(BF16)

    o_ref[...] = jnp.dot(css_ref[...], ab_ref[...], preferred_element_type=F32)


def _dft_direct(zf, nseq, s, gd, csc):
    w_f = zf.shape[1]
    tm = TOKEN_TILE
    cs, sn = _dft_tables(s)
    scale = 1.0 / math.sqrt(s * gd)
    css = (jnp.concatenate([cs, -sn], axis=1) * scale).astype(BF16)
    return pl.pallas_call(
        functools.partial(_dft_direct_kernel, gd=gd),
        grid=(nseq, s // tm),
        in_specs=[
            pl.BlockSpec((s, w_f), lambda b, i: (b, 0)),
            _const_spec((gd, 2 * gd)),
            pl.BlockSpec((tm, 2 * s), lambda b, i: (i, 0)),
        ],
        out_specs=pl.BlockSpec((tm, w_f), lambda b, i: (b * (s // tm) + i, 0)),
        out_shape=jax.ShapeDtypeStruct((nseq * s, w_f), F32),
        scratch_shapes=[pltpu.VMEM((2 * s, w_f), BF16)],
        compiler_params=_params("parallel", "arbitrary"),
        name="dft_direct",
    )(zf, csc, css)


def _dft_stage1_kernel(x_ref, csc_ref, m1_ref, twc_ref, tws_ref, tr_ref, ti_ref,
                       x2, tr2, ti2, xs_ref, ab_ref, *, n1, gd):
    nf = x_ref.shape[1]
    for g in range(x_ref.shape[2] // gd):
        cols = slice(g * gd, (g + 1) * gd)
        x2[...] = x_ref[:, :, cols].reshape(n1 * nf, gd)
        for f in range(nf):
            xs_ref[f * n1:(f + 1) * n1, :] = x2[pl.ds(f, n1, stride=nf), :].astype(BF16)
        ab = jnp.dot(xs_ref[...], csc_ref[...], preferred_element_type=F32)
        for f in range(nf):
            ab_ref[0:n1, f * gd:(f + 1) * gd] = ab[f * n1:(f + 1) * n1, :gd].astype(BF16)
            ab_ref[n1:2 * n1, f * gd:(f + 1) * gd] = ab[f * n1:(f + 1) * n1, gd:].astype(BF16)
        gq = jnp.dot(m1_ref[...], ab_ref[...], preferred_element_type=F32)
        for f in range(nf):
            gr = gq[0:n1, f * gd:(f + 1) * gd]
            gi = gq[n1:2 * n1, f * gd:(f + 1) * gd]
            tc = twc_ref[f]
            ts = tws_ref[f]
            tr2[pl.ds(f, n1, stride=nf), :] = gr * tc + gi * ts
            ti2[pl.ds(f, n1, stride=nf), :] = gi * tc - gr * ts
        tr_ref[:, :, cols] = tr2[...].reshape(n1, nf, gd)
        ti_ref[:, :, cols] = ti2[...].reshape(n1, nf, gd)


def _dft_stage2_kernel(tr_ref, ti_ref, c2s2_ref, o_ref, o2, tcat_ref, *, n2, gd):
    nk = tr_ref.shape[0]
    for g in range(tr_ref.shape[2] // gd):
        cols = slice(g * gd, (g + 1) * gd)
        for q in range(nk):
            tcat_ref[0:n2, q * gd:(q + 1) * gd] = tr_ref[q, :, cols].astype(BF16)
            tcat_ref[n2:2 * n2, q * gd:(q + 1) * gd] = ti_ref[q, :, cols].astype(BF16)
        y = jnp.dot(c2s2_ref[...], tcat_ref[...], preferred_element_type=F32)
        for q in range(nk):
            o2[pl.ds(q, n2, stride=nk), :] = y[:, q * gd:(q + 1) * gd]
        o_ref[:, :, cols] = o2[...].reshape(n2, nk, gd)


def _dft_two_stage(zf, row0, nseq, s, gd, csc):
    w_f = zf.shape[1]
    n1 = DFT_RADIX
    n2 = s // n1
    nf = V7X_SUBLANES
    assert s % n1 == 0 and n2 % nf == 0 and n1 % nf == 0 and gd == V7X_LANES
    if row0 % s == 0:
        blk0 = row0 // s
    else:
        zf, blk0 = zf[row0:], 0
    scale = 1.0 / math.sqrt(s * gd)

    c1, s1 = _dft_tables(n1)
    m1 = jnp.concatenate([jnp.concatenate([c1, -s1], axis=1),
                          jnp.concatenate([-s1, -c1], axis=1)], axis=0).astype(BF16)
    a = lax.broadcasted_iota(jnp.int32, (n2, n1), 0)
    b = lax.broadcasted_iota(jnp.int32, (n2, n1), 1)
    ang = ((a * b) % s).astype(F32) * (2.0 * math.pi / s)
    twc = jnp.broadcast_to((jnp.cos(ang) * scale)[:, :, None], (n2, n1, gd))
    tws = jnp.broadcast_to((jnp.sin(ang) * scale)[:, :, None], (n2, n1, gd))
    c2, s2 = _dft_tables(n2)
    c2s2 = jnp.concatenate([c2, s2], axis=1).astype(BF16)

    x3 = zf.reshape(zf.shape[0] // n2, n2, w_f)
    t_shape = jax.ShapeDtypeStruct((nseq * n1, n2, w_f), F32)
    tr, ti = pl.pallas_call(
        functools.partial(_dft_stage1_kernel, n1=n1, gd=gd),
        grid=(nseq, n2 // nf),
        in_specs=[
            pl.BlockSpec((n1, nf, w_f), lambda b, f: (blk0 + b, f, 0)),
            _const_spec((gd, 2 * gd)),
            _const_spec((2 * n1, 2 * n1)),
            pl.BlockSpec((nf, n1, gd), lambda b, f: (f, 0, 0)),
            pl.BlockSpec((nf, n1, gd), lambda b, f: (f, 0, 0)),
        ],
        out_specs=[
            pl.BlockSpec((n1, nf, w_f), lambda b, f: (b, f, 0)),
            pl.BlockSpec((n1, nf, w_f), lambda b, f: (b, f, 0)),
        ],
        out_shape=[t_shape, t_shape],
        scratch_shapes=[pltpu.VMEM((n1 * nf, gd), F32), pltpu.VMEM((n1 * nf, gd), F32),
                        pltpu.VMEM((n1 * nf, gd), F32),
                        pltpu.VMEM((nf * n1, gd), BF16), pltpu.VMEM((2 * n1, nf * gd), BF16)],
        compiler_params=_params("parallel", "parallel"),
        name="dft_stage1",
    )(x3, csc, m1, twc, tws)

    y = pl.pallas_call(
        functools.partial(_dft_stage2_kernel, n2=n2, gd=gd),
        grid=(nseq, n1 // nf),
        in_specs=[
            pl.BlockSpec((nf, n2, w_f), lambda b, q: (b * (n1 // nf) + q, 0, 0)),
            pl.BlockSpec((nf, n2, w_f), lambda b, q: (b * (n1 // nf) + q, 0, 0)),
            _const_spec((n2, 2 * n2)),
        ],
        out_specs=pl.BlockSpec((n2, nf, w_f), lambda b, q: (b, q, 0)),
        out_shape=jax.ShapeDtypeStruct((nseq * n2, n1, w_f), F32),
        scratch_shapes=[pltpu.VMEM((n2 * nf, gd), F32), pltpu.VMEM((2 * n2, nf * gd), BF16)],
        compiler_params=_params("parallel", "parallel"),
        name="dft_stage2",
    )(tr, ti, c2s2)
    return y.reshape(nseq * s, w_f)


def _mixer_out_kernel(edge_ref, yfa_ref, yfb_ref, zr_ref, pc_ref, pv_ref, nc_ref, nv_ref, *rest,
                      n_a, w_c, d, x_starts):
    x_refs = rest[:len(x_starts)]
    cw_ref, cb_ref, wf_ref, wc_ref, wo_ref, o_ref = rest[len(x_starts):]
    i = pl.program_id(0)
    tm = o_ref.shape[0]
    halo = pc_ref.shape[0]
    x = x_refs[0][...]
    for x_ref, first in zip(x_refs[1:], x_starts[1:]):
        x = jnp.where(i >= first, x_ref[...], x)
    zb = zr_ref[:, 0:w_c].astype(F32)
    u = zr_ref[:, w_c:2 * w_c].astype(F32) * zr_ref[:, 2 * w_c:3 * w_c].astype(F32)
    u_prev = pc_ref[halo - 1:halo, :].astype(F32) * pv_ref[halo - 1:halo, :].astype(F32)
    u_next = nc_ref[0:1, :].astype(F32) * nv_ref[0:1, :].astype(F32)
    u_prev = jnp.where(edge_ref[i, 0] > 0, u_prev, 0.0)
    u_next = jnp.where(edge_ref[i, 1] > 0, u_next, 0.0)
    row = lax.broadcasted_iota(jnp.int32, (tm, w_c), 0)
    u_m1 = jnp.where(row == 0, u_prev, pltpu.roll(u, 1, 0))
    u_p1 = jnp.where(row == tm - 1, u_next, pltpu.roll(u, tm - 1, 0))
    cv = u_m1 * cw_ref[0:1, :] + u * cw_ref[1:2, :] + u_p1 * cw_ref[2:3, :] + cb_ref[...]
    yc = (zb * cv).astype(BF16)
    yf = jnp.where(i < n_a, yfa_ref[...], yfb_ref[...]).astype(BF16)
    gf = jax.nn.sigmoid(zr_ref[:, 3 * w_c:3 * w_c + d].astype(F32))
    gc = jax.nn.sigmoid(zr_ref[:, 3 * w_c + d:3 * w_c + 2 * d].astype(F32))
    m = (gf * jnp.dot(yf, wf_ref[...], preferred_element_type=F32)
         + gc * jnp.dot(yc, wc_ref[...], preferred_element_type=F32))
    o_ref[...] = x + jnp.dot(m.astype(BF16), wo_ref[...], preferred_element_type=F32)


def _mixer_out(seq_bounds, yf_a, yf_b, zr, x_parts, conv_w, conv_b, w_fourier, w_conv_out, w_o):
    d = x_parts[0].shape[1]
    t = sum(p.shape[0] for p in x_parts)
    w_f = w_fourier.shape[0]
    w_c = w_conv_out.shape[0]
    tm = MIXER_TILE
    halo = 2 * V7X_SUBLANES
    n_a = yf_a.shape[0] // tm
    n_b = yf_b.shape[0] // tm
    nh = t // halo
    hb = tm // halo
    assert w_c % V7X_LANES == 0 and t % tm == 0 and all(b % tm == 0 for b in seq_bounds)
    edges = jnp.asarray([[0 if i * tm in seq_bounds else 1, 0 if (i + 1) * tm in seq_bounds else 1]
                         for i in range(t // tm)], dtype=jnp.int32)
    assert all(p.shape[0] % tm == 0 for p in x_parts)
    x_tiles = [p.shape[0] // tm for p in x_parts]
    x_starts = tuple(sum(x_tiles[:k]) for k in range(len(x_parts)))
    x_specs = [pl.BlockSpec((tm, d), lambda i, e, first=first, n=n: (jnp.clip(i - first, 0, n - 1), 0))
               for first, n in zip(x_starts, x_tiles)]
    return pl.pallas_call(
        functools.partial(_mixer_out_kernel, n_a=n_a, w_c=w_c, d=d, x_starts=x_starts),
        grid_spec=pltpu.PrefetchScalarGridSpec(
            num_scalar_prefetch=1,
            grid=(t // tm,),
            in_specs=[
                pl.BlockSpec((tm, w_f), lambda i, e: (jnp.minimum(i, n_a - 1), 0)),
                pl.BlockSpec((tm, w_f), lambda i, e: (jnp.clip(i - n_a, 0, n_b - 1), 0)),
                pl.BlockSpec((tm, zr.shape[1]), lambda i, e: (i, 0)),
                pl.BlockSpec((halo, w_c), lambda i, e: (jnp.maximum(i * hb - 1, 0), 1)),
                pl.BlockSpec((halo, w_c), lambda i, e: (jnp.maximum(i * hb - 1, 0), 2)),
                pl.BlockSpec((halo, w_c), lambda i, e: (jnp.minimum((i + 1) * hb, nh - 1), 1)),
                pl.BlockSpec((halo, w_c), lambda i, e: (jnp.minimum((i + 1) * hb, nh - 1), 2)),
                *x_specs,
                _const_spec((conv_w.shape[0], w_c)),
                _const_spec((1, w_c)),
                _const_spec((w_f, d)),
                _const_spec((w_c, d)),
                _const_spec((d, d)),
            ],
            out_specs=pl.BlockSpec((tm, d), lambda i, e: (i, 0)),
        ),
        out_shape=jax.ShapeDtypeStruct((t, d), F32),
        compiler_params=_params("parallel"),
        name="mixer_out",
    )(edges, yf_a, yf_b, zr, zr, zr, zr, zr, *x_parts, conv_w, conv_b.reshape(1, w_c),
      w_fourier, w_conv_out, w_o)


def _dense_ffn_kernel(*refs, final_norm, n_cast):
    x_ref, g_ref, wg_ref, wu_ref, wd_ref, fg_ref = refs[:6]
    cast_in = refs[6:6 + n_cast]
    o_ref = refs[6 + n_cast]
    cast_out = refs[7 + n_cast:7 + 2 * n_cast]
    h_ref = refs[7 + 2 * n_cast]
    j = pl.program_id(1)

    @pl.when(j == 0)
    def _():
        h_ref[...] = _rmsnorm(x_ref[...], g_ref[...]).astype(BF16)
        o_ref[...] = x_ref[...]

    h = h_ref[...]
    a = jnp.dot(h, wg_ref[...], preferred_element_type=F32)
    b = jnp.dot(h, wu_ref[...], preferred_element_type=F32)
    act = (a * jax.nn.sigmoid(a) * b).astype(BF16)
    o_ref[...] += jnp.dot(act, wd_ref[...], preferred_element_type=F32)

    if final_norm:
        @pl.when(j == pl.num_programs(1) - 1)
        def _():
            o_ref[...] = _rmsnorm(o_ref[...], fg_ref[...])

    for src, dst in zip(cast_in, cast_out):
        dst[...] = src[...].astype(BF16)


def _cast_chunk_rows(rows, n_steps):
    step = 2 * V7X_SUBLANES
    chunk = step
    while rows % chunk or rows // chunk > n_steps:
        chunk += step
        assert chunk <= rows
    return chunk


def _dense_ffn(x, g, wg, wu, wd, final_g, cast_jobs=()):
    t, d = x.shape
    f = wg.shape[1]
    tm, tf = TOKEN_TILE, FFN_COLS
    assert t % tm == 0 and f % tf == 0
    nj = f // tf
    n_steps = (t // tm) * nj
    final_norm = final_g is not None
    fg = (final_g if final_norm else jnp.ones((d,), F32)).reshape(1, d)

    def cast_specs():
        specs = []
        for w in cast_jobs:
            rows, cols = w.shape
            chunk = _cast_chunk_rows(rows, n_steps)
            last = rows // chunk - 1
            specs.append(pl.BlockSpec(
                (chunk, cols), lambda i, j, last=last: (jnp.minimum(i * nj + j, last), 0)))
        return specs

    outs = pl.pallas_call(
        functools.partial(_dense_ffn_kernel, final_norm=final_norm, n_cast=len(cast_jobs)),
        grid=(t // tm, nj),
        in_specs=[
            pl.BlockSpec((tm, d), lambda i, j: (i, 0)),
            pl.BlockSpec((1, d), lambda i, j: (0, 0)),
            pl.BlockSpec((d, tf), lambda i, j: (0, j)),
            pl.BlockSpec((d, tf), lambda i, j: (0, j)),
            pl.BlockSpec((tf, d), lambda i, j: (j, 0)),
            pl.BlockSpec((1, d), lambda i, j: (0, 0)),
        ] + cast_specs(),
        out_specs=[pl.BlockSpec((tm, d), lambda i, j: (i, 0))] + cast_specs(),
        out_shape=[jax.ShapeDtypeStruct((t, d), F32)]
        + [jax.ShapeDtypeStruct(w.shape, BF16) for w in cast_jobs],
        scratch_shapes=[pltpu.VMEM((tm, d), BF16)],
        compiler_params=_params("arbitrary", "arbitrary"),
        name="dense_ffn",
    )(x, g.reshape(1, d), wg, wu, wd, fg, *cast_jobs)
    return outs[0], list(outs[1:])


def _router_kernel(x_ref, g_ref, rwh_ref, rwl_ref, tri_ref, ri_ref, rp_ref, cnt_ref, carry_ref, *, n_exp):
    @pl.when(pl.program_id(0) == 0)
    def _():
        carry_ref[...] = jnp.zeros_like(carry_ref)

    tm = x_ref.shape[0]
    lanes = rwh_ref.shape[1]
    h = _rmsnorm(x_ref[...], g_ref[...])
    h_hi = h.astype(BF16)
    h_lo = (h - h_hi.astype(F32)).astype(BF16)
    both = jnp.dot(h_hi, jnp.concatenate([rwh_ref[...], rwl_ref[...]], axis=1),
                   preferred_element_type=F32)
    logits = both[:, :lanes] + both[:, lanes:] + jnp.dot(h_lo, rwh_ref[...], preferred_element_type=F32)
    lane = lax.broadcasted_iota(jnp.int32, (tm, lanes), 1).astype(F32)
    neg = jnp.float32(-jnp.inf)
    lg = jnp.where(lane < n_exp, logits, neg)
    m1 = jnp.max(lg, axis=1, keepdims=True)
    i1 = jnp.min(jnp.where(lg == m1, lane, float(lanes)), axis=1, keepdims=True)
    lg2 = jnp.where(lane == i1, neg, lg)
    m2 = jnp.max(lg2, axis=1, keepdims=True)
    i2 = jnp.min(jnp.where(lg2 == m2, lane, float(lanes)), axis=1, keepdims=True)
    sel = jnp.where((lane == i1) | (lane == i2), 1.0, 0.0)
    before = jnp.dot(tri_ref[...], sel.astype(BF16), preferred_element_type=F32)
    rank = carry_ref[...] + before
    r1 = jnp.sum(jnp.where(lane == i1, rank, 0.0), axis=1, keepdims=True)
    r2 = jnp.sum(jnp.where(lane == i2, rank, 0.0), axis=1, keepdims=True)
    carry_ref[...] += jnp.sum(sel, axis=0, keepdims=True)
    cnt_ref[...] = carry_ref[...]
    e21 = jnp.exp(m2 - m1)
    p1 = 1.0 / (1.0 + e21)
    p2 = e21 / (1.0 + e21)
    rp_ref[...] = jnp.where(lane == 0, p1, jnp.where(lane == 1, p2, 0.0))
    q = jnp.where(lane == 0, i1, jnp.where(lane == 1, i2, jnp.where(lane == 2, r1, jnp.where(lane == 3, r2, 0.0))))
    ri_ref[...] = q.T[0:ri_ref.shape[0], :].astype(jnp.int32)


def _router(x, g, router_w):
    t, d = x.shape
    n_exp = router_w.shape[1]
    tm = TOKEN_TILE
    lanes = V7X_LANES
    rw = jnp.zeros((d, lanes), F32).at[:, :n_exp].set(router_w)
    rw_hi = rw.astype(BF16)
    rw_lo = (rw - rw_hi.astype(F32)).astype(BF16)
    tri =(lax.broadcasted_iota(jnp.int32, (tm, tm), 1)
           < lax.broadcasted_iota(jnp.int32, (tm, tm), 0)).astype(BF16)
    return pl.pallas_call(
        functools.partial(_router_kernel, n_exp=n_exp),
        grid=(t // tm,),
        in_specs=[
            pl.BlockSpec((tm, d), lambda i: (i, 0)),
            _const_spec((1, d)),
            _const_spec((d, lanes)),
            _const_spec((d, lanes)),
            _const_spec((tm, tm)),
        ],
        out_specs=[
            pl.BlockSpec((V7X_SUBLANES, tm), lambda i: (0, i)),
            pl.BlockSpec((tm, lanes), lambda i: (i, 0)),
            pl.BlockSpec((1, lanes), lambda i: (0, 0)),
        ],
        out_shape=[
            jax.ShapeDtypeStruct((V7X_SUBLANES, t), jnp.int32),
            jax.ShapeDtypeStruct((t, lanes), F32),
            jax.ShapeDtypeStruct((1, lanes), F32),
        ],
        scratch_shapes=[pltpu.VMEM((1, lanes), F32)],
        compiler_params=_params("arbitrary"),
        name="router",
    )(x, g.reshape(1, d), rw_hi, rw_lo, tri)


def _row_copy(src_ref, src_row, dst_ref, dst_row, sem):
    return pltpu.make_async_copy(src_ref.at[pl.ds(src_row, 1), :], dst_ref.at[pl.ds(dst_row, 1), :], sem)


def _dispatch_kernel(tend_ref, nv_ref, x_ref, g_ref, slot_ref, hs_ref, h_ref, sem, zsem, *, n_tiles):
    tm = x_ref.shape[0]

    @pl.when(pl.program_id(0) == 0)
    def _():
        h_ref[...] = jnp.zeros_like(h_ref)

        def zero_tile(tile):
            cp = pltpu.make_async_copy(h_ref, hs_ref.at[pl.ds(tile * tm, tm), :], zsem)
            cp.start()
            cp.wait()

        for e in range(tend_ref.shape[0]):
            first = tend_ref[e - 1] if e else 0

            @pl.when(tend_ref[e] > first)
            def _():
                zero_tile(tend_ref[e] - 1)

        def tail(tile, c):
            zero_tile(tile)
            return c

        lax.fori_loop(nv_ref[0], n_tiles, tail, 0)

    h_ref[...] = _rmsnorm(x_ref[...], g_ref[...])

    def issue(s, c):
        for k in range(TOP_K):
            _row_copy(h_ref, s, hs_ref, slot_ref[k * tm + s], sem).start(priority=k % 2)
        return c

    lax.fori_loop(0, tm, issue, 0, unroll=DMA_ISSUE_UNROLL)
    for _ in range(TOP_K):
        pltpu.make_async_copy(h_ref, hs_ref.at[pl.ds(0, tm), :], sem).wait()


def _dispatch(x, g, slots, tile_end, n_valid, n_tiles):
    t, d = x.shape
    tm = TOKEN_TILE
    assert tm == EXPERT_TILE
    return pl.pallas_call(
        functools.partial(_dispatch_kernel, n_tiles=n_tiles),
        grid_spec=pltpu.PrefetchScalarGridSpec(
            num_scalar_prefetch=2,
            grid=(t // tm,),
            in_specs=[
                pl.BlockSpec((tm, d), lambda i, *_: (i, 0)),
                _const_spec((1, d)),
                pl.BlockSpec((TOP_K * tm,), lambda i, *_: (i,), memory_space=pltpu.SMEM),
            ],
            out_specs=pl.BlockSpec(memory_space=pl.ANY),
            scratch_shapes=[pltpu.VMEM((tm, d), F32), pltpu.SemaphoreType.DMA(()),
                            pltpu.SemaphoreType.DMA(())],
        ),
        out_shape=jax.ShapeDtypeStruct((n_tiles * EXPERT_TILE, d), F32),
        compiler_params=_params("arbitrary"),
        name="dispatch",
    )(tile_end, n_valid, x, g.reshape(1, d), slots)


def _expert_ffn_kernel(te_ref, nv_ref, hs_ref, wg_ref, wu_ref, wd_ref, o_ref, xb_ref):
    del te_ref
    j = pl.program_id(1)
    used = pl.program_id(0) < nv_ref[0]

    @pl.when(jnp.logical_and(jnp.logical_not(used), j == 0))
    def _():
        o_ref[...] = jnp.zeros_like(o_ref)

    @pl.when(used)
    def _():
        @pl.when(j == 0)
        def _():
            xb_ref[...] = hs_ref[...].astype(BF16)
            o_ref[...] = jnp.zeros_like(o_ref)

        xb = xb_ref[...]
        a = jnp.dot(xb, wg_ref[...], preferred_element_type=F32)
        b = jnp.dot(xb, wu_ref[...], preferred_element_type=F32)
        act = (a * jax.nn.sigmoid(a) * b).astype(BF16)
        o_ref[...] += jnp.dot(act, wd_ref[...], preferred_element_type=F32)


def _expert_ffn(hs, tile_expert, n_valid, wg, wu, wd):
    n_rows, d = hs.shape
    f = wg.shape[2]
    tm, tf = EXPERT_TILE, EXPERT_COLS
    assert n_rows % tm == 0 and f % tf == 0
    nj = f // tf

    def col(g, j, nv):
        return jnp.where(g < nv[0], j, nj - 1)

    return pl.pallas_call(
        _expert_ffn_kernel,
        grid_spec=pltpu.PrefetchScalarGridSpec(
            num_scalar_prefetch=2,
            grid=(n_rows // tm, nj),
            in_specs=[
                pl.BlockSpec((tm, d), lambda g, j, te, nv: (g, 0)),
                pl.BlockSpec((None, d, tf), lambda g, j, te, nv: (te[g], 0, col(g, j, nv))),
                pl.BlockSpec((None, d, tf), lambda g, j, te, nv: (te[g], 0, col(g, j, nv))),
                pl.BlockSpec((None, tf, d), lambda g, j, te, nv: (te[g], col(g, j, nv), 0)),
            ],
            out_specs=pl.BlockSpec((tm, d), lambda g, j, te, nv: (g, 0)),
            scratch_shapes=[pltpu.VMEM((tm, d), BF16)],
        ),
        out_shape=jax.ShapeDtypeStruct((n_rows, d), F32),
        compiler_params=_params("arbitrary", "arbitrary"),
        name="expert_ffn",
    )(tile_expert, n_valid, hs, wg, wu, wd)


def _combine_kernel(x_ref, slot_ref, next_slot_ref, rp_ref, fg_ref, ys_ref, o_ref, y_ref, sem, *,
                    final_norm):
    tm = x_ref.shape[0]
    i = pl.program_id(0)
    cur = i % 2

    def gather(slots, buf):
        def issue(s, c):
            for k in range(TOP_K):
                _row_copy(ys_ref, slots[k * tm + s], y_ref.at[buf * TOP_K + k], s,
                          sem.at[buf]).start(priority=k % 2)
            return c

        lax.fori_loop(0, tm, issue, 0, unroll=DMA_ISSUE_UNROLL)

    @pl.when(i == 0)
    def _():
        gather(slot_ref, cur)

    @pl.when(i + 1 < pl.num_programs(0))
    def _():
        gather(next_slot_ref, 1 - cur)

    for k in range(TOP_K):
        pltpu.make_async_copy(ys_ref.at[pl.ds(0, tm), :], y_ref.at[cur * TOP_K + k], sem.at[cur]).wait()

    y = (x_ref[...] + rp_ref[:, 0:1] * y_ref[cur * TOP_K]
         + rp_ref[:, 1:2] * y_ref[cur * TOP_K + 1])
    if final_norm:
        y = _rmsnorm(y, fg_ref[...])
    o_ref[...] = y


def _combine(x, slots, route_p, ys, final_g, row0, rows):
    t, d = x.shape
    tm = TOKEN_TILE
    assert row0 % tm == 0 and rows % tm == 0
    b0 = row0 // tm
    last = b0 + rows // tm - 1
    final_norm = final_g is not None
    fg = (final_g if final_norm else jnp.ones((d,), F32)).reshape(1, d)
    return pl.pallas_call(
        functools.partial(_combine_kernel, final_norm=final_norm),
        grid=(rows // tm,),
        in_specs=[
            pl.BlockSpec((tm, d), lambda i: (b0 + i, 0)),
            pl.BlockSpec((TOP_K * tm,), lambda i: (b0 + i,), memory_space=pltpu.SMEM),
            pl.BlockSpec((TOP_K * tm,), lambda i: (jnp.minimum(b0 + i + 1, last),),
                         memory_space=pltpu.SMEM),
            pl.BlockSpec((tm, route_p.shape[1]), lambda i: (b0 + i, 0)),
            _const_spec((1, d)),
            pl.BlockSpec(memory_space=pl.ANY),
        ],
        out_specs=pl.BlockSpec((tm, d), lambda i: (i, 0)),
        scratch_shapes=[pltpu.VMEM((2 * TOP_K, tm, d), F32), pltpu.SemaphoreType.DMA((2,))],
        out_shape=jax.ShapeDtypeStruct((rows, d), F32),
        compiler_params=_params("arbitrary"),
        name="combine",
    )(x, slots, slots, route_p, fg, ys)


def _moe_plan(route_i, counts, n_exp, n_tiles):
    tm = EXPERT_TILE
    cnt = counts[0, :n_exp].astype(jnp.int32)
    tiles = (cnt + tm - 1) // tm
    tile_end = jnp.cumsum(tiles)
    offsets = (tile_end - tiles) * tm
    n_valid = tile_end[-1]
    g = jnp.arange(n_tiles, dtype=jnp.int32)
    gc = jnp.minimum(g, n_valid - 1)
    tile_expert = jnp.sum((gc[:, None] >= tile_end[None, :]).astype(jnp.int32), axis=1)
    expert_ids = jnp.arange(n_exp, dtype=jnp.int32)[None, :, None]
    chosen = route_i[0:TOP_K, None, :] == expert_ids
    slots = jnp.sum(jnp.where(chosen, offsets[None, :, None], 0), axis=1) + route_i[TOP_K:2 * TOP_K]
    slots = slots.reshape(TOP_K, -1, TOKEN_TILE).transpose(1, 0, 2).reshape(-1)
    return slots, tile_end, tile_expert, n_valid.reshape(1)


def _moe_layer(x, g, router_w, wg, wu, wd, final_g, splits):
    t, d = x.shape
    n_exp = router_w.shape[1]
    n_tiles = (t * TOP_K) // EXPERT_TILE + n_exp
    route_i, route_p, counts = _router(x, g, router_w)
    slots, tile_end, tile_expert, n_valid = _moe_plan(route_i, counts, n_exp, n_tiles)
    hs = _dispatch(x, g, slots, tile_end, n_valid, n_tiles)
    ys = _expert_ffn(hs, tile_expert, n_valid, wg, wu, wd)
    return [_combine(x, slots, route_p, ys, final_g, r0, n) for r0, n in splits]


def kernel(x_prompt, x_sample, norm1_g, w_in, conv_w, conv_b, w_fourier, w_conv_out, w_o, norm2_g,
           dense_w_gate, dense_w_up, dense_w_down, router_w, moe_w_gate, moe_w_up, moe_w_down, final_g):
    bp, sp, d = x_prompt.shape
    bs, ss, _ = x_sample.shape
    tp, ts = bp * sp, bs * ss
    t = tp + ts
    depth = norm1_g.shape[0]
    w_f = w_fourier.shape[1]
    gd = w_f // F_GROUPS
    tm = TOKEN_TILE
    assert sp % tm == 0 and ss % tm == 0

    x_parts = [x_prompt.reshape(tp, d), x_sample.reshape(ts, d)]

    seq_bounds = frozenset([b * sp for b in range(bp + 1)] + [tp + b * ss for b in range(bs + 1)])

    cc, sc = _dft_tables(gd)
    csc = jnp.concatenate([cc, sc], axis=1).astype(BF16)

    for l in range(depth):
        if l:
            x_parts = [x]
        zf, zr = _norm_inproj(x_parts, norm1_g[l], w_in[l].astype(BF16), w_f)
        yf_p = _dft_direct(zf, bp, sp, gd, csc)
        yf_s = _dft_two_stage(zf, tp, bs, ss, gd, csc)
        x = _mixer_out(seq_bounds, yf_p, yf_s, zr, x_parts, conv_w[l], conv_b[l], w_fourier[l].astype(BF16),
                       w_conv_out[l].astype(BF16), w_o[l].astype(BF16))
        last = l == depth - 1
        fg = final_g if last else None
        j = l // 2
        if l % 2 == 0:
            jobs = [] if last else [w[j].reshape(-1, w.shape[-1])
                                    for w in (moe_w_gate, moe_w_up, moe_w_down)]
            x, moe_bf16 = _dense_ffn(x, norm2_g[l], dense_w_gate[j].astype(BF16),
                                     dense_w_up[j].astype(BF16), dense_w_down[j].astype(BF16), fg, jobs)
            if last:
                outs = [x[:tp], x[tp:]]
        else:
            splits = [(0, tp), (tp, ts)] if last else [(0, t)]
            wg, wu, wd = (wb.reshape(w.shape[1:])
                          for wb, w in zip(moe_bf16, (moe_w_gate, moe_w_up, moe_w_down)))
            outs = _moe_layer(x, norm2_g[l], router_w[j], wg, wu, wd, fg, splits)
            if not last:
                x = outs[0]
    return outs[0].reshape(bp, sp, d), outs[1].reshape(bs, ss, d)
```
